```python
import jax, jax.numpy as jnp
from jax import lax
import numpy as np

D_MODEL = 1024
BATCH = 4
SEQ = 8192
DEPTH = 2

HEAD_DIM = 64
A_Q_HEADS = 8
A_KV_HEADS = 2
A_BLOCK = 128
WINDOW = 128
ROPE_THETA = 10000.0
B_HEADS = 8
GRID_W = 64
NA_MAX_KH = 8
NA_KW = 16
NA_QBLOCK_W = 16
NA_SPAN_W = 2 * NA_KW
A_Q_DIM = A_Q_HEADS * HEAD_DIM
A_KV_DIM = A_KV_HEADS * HEAD_DIM
B_DIM = B_HEADS * HEAD_DIM
ATTN_IN = A_Q_DIM + 2 * A_KV_DIM + 3 * B_DIM
ATTN_OUT = A_Q_DIM + B_DIM
SSM_D_INNER = 2 * D_MODEL
SSM_HEAD_DIM = 64
SSM_HEADS = SSM_D_INNER // SSM_HEAD_DIM
SSM_GROUPS = 8
SSM_STATE = 128
SSM_CONV = 5
SSM_CHUNK = 128
SSM_CONV_DIM = SSM_D_INNER + 2 * SSM_GROUPS * SSM_STATE
SSM_IN = SSM_D_INNER + SSM_CONV_DIM + 2 * SSM_HEADS
D_FF = 4 * D_MODEL
DN_ALPHA = (2 * DEPTH) ** 0.25
DN_BETA = (8 * DEPTH) ** -0.25
LN_EPS = 1e-5
RMS_EPS = 1e-5
NEG_INF = -1e30
N_ATTN_LAYERS = (DEPTH + 1) // 2
N_SSM_LAYERS = DEPTH // 2

kernel_name = "hybrid_window_natten_ssd_encoder"


def layer_norm(x, g, b):
    xf = x.astype(jnp.float32)
    mu = jnp.mean(xf, -1, keepdims=True)
    var = jnp.mean(jnp.square(xf - mu), -1, keepdims=True)
    return ((xf - mu) * lax.rsqrt(var + LN_EPS)).astype(x.dtype) * g + b


def rope(x, pos):
    half = HEAD_DIM // 2
    inv = ROPE_THETA ** (-jnp.arange(half, dtype=jnp.float32) / half)
    ang = pos.astype(jnp.float32)[:, None] * inv[None, :]
    cos = jnp.cos(ang)[None, :, None, :].astype(x.dtype)
    sin = jnp.sin(ang)[None, :, None, :].astype(x.dtype)
    x1, x2 = x[..., :half], x[..., half:]
    return jnp.concatenate([x1 * cos - x2 * sin, x2 * cos + x1 * sin], axis=-1)


def windowed_gqa_sink(q, k, v, sink):
    bsz, t = q.shape[:2]
    nb = t // A_BLOCK
    rep = A_Q_HEADS // A_KV_HEADS
    qb = q.reshape(bsz, nb, A_BLOCK, A_KV_HEADS, rep, HEAD_DIM)

    def band(u):
        ub = u.reshape(bsz, nb, A_BLOCK, A_KV_HEADS, HEAD_DIM)
        up = jnp.pad(ub, ((0, 0), (1, 1), (0, 0), (0, 0), (0, 0)))
        return jnp.concatenate([up[:, :-2], up[:, 1:-1], up[:, 2:]], axis=2)

    kb, vb = band(k), band(v)
    s = jnp.einsum('bnqgrd,bnkgd->bngrqk', qb, kb).astype(jnp.float32) * (HEAD_DIM ** -0.5)
    qi = np.arange(A_BLOCK)[:, None] + A_BLOCK
    ki = np.arange(3 * A_BLOCK)[None, :]
    kabs = (np.arange(nb)[:, None, None] - 1) * A_BLOCK + ki[None]
    valid = (np.abs(qi - ki) <= WINDOW)[None] & (kabs >= 0) & (kabs < t)
    s = jnp.where(jnp.asarray(valid)[None, :, None, None], s, NEG_INF)
    sink_l = jnp.broadcast_to(sink.astype(jnp.float32).reshape(A_KV_HEADS, rep)[None, None, :, :, None, None],
                              s.shape[:-1] + (1,))
    p = jax.nn.softmax(jnp.concatenate([s, sink_l], axis=-1), axis=-1)[..., :-1]
    o = jnp.einsum('bngrqk,bnkgd->bnqgrd', p.astype(v.dtype), vb)
    return o.reshape(bsz, t, A_Q_DIM)


def _na_static():
    ncb = GRID_W // NA_QBLOCK_W
    starts = [min(max(j * NA_QBLOCK_W - NA_KW // 2, 0), GRID_W - NA_SPAN_W) for j in range(ncb)]
    c = np.arange(GRID_W).reshape(ncb, NA_QBLOCK_W)
    cs = np.clip(c - NA_KW // 2, 0, GRID_W - NA_KW)
    kc = np.array(starts)[:, None] + np.arange(NA_SPAN_W)[None, :]
    mask = (kc[:, None, :] >= cs[:, :, None]) & (kc[:, None, :] < cs[:, :, None] + NA_KW)
    dc = np.clip(kc[:, None, :] - c[:, :, None] + NA_KW - 1, 0, 2 * NA_KW - 2)
    return starts, mask, dc


def neighborhood_attention(q, k, v, rpb):
    bsz, t = q.shape[:2]
    rows = t // GRID_W
    kh = min(NA_MAX_KH, rows)
    ncb = GRID_W // NA_QBLOCK_W
    starts, mask, dc = _na_static()
    qg = q.reshape(bsz, rows, ncb, NA_QBLOCK_W, B_HEADS, HEAD_DIM)
    kg = k.reshape(bsz, rows, GRID_W, B_HEADS, HEAD_DIM)
    vg = v.reshape(bsz, rows, GRID_W, B_HEADS, HEAD_DIM)
    mask_j = jnp.asarray(mask)[:, :, None, :]
    scale = HEAD_DIM ** -0.5

    def row_step(r):
        rs = jnp.clip(r - kh // 2, 0, rows - kh)
        k_rows = lax.dynamic_slice_in_dim(kg, rs, kh, axis=1)
        v_rows = lax.dynamic_slice_in_dim(vg, rs, kh, axis=1)
        k_blk = jnp.stack([k_rows[:, :, s0:s0 + NA_SPAN_W] for s0 in starts], axis=1)
        v_blk = jnp.stack([v_rows[:, :, s0:s0 + NA_SPAN_W] for s0 in starts], axis=1)
        q_r = lax.dynamic_index_in_dim(qg, r, axis=1, keepdims=False)
        s = jnp.einsum('bjqhd,bjakhd->bhjqak', q_r, k_blk).astype(jnp.float32) * scale
        dr = rs + jnp.arange(kh) - r + NA_MAX_KH - 1
        bias = rpb[:, dr][:, :, dc].transpose(0, 2, 3, 1, 4)
        s = jnp.where(mask_j, s + bias.astype(jnp.float32), NEG_INF)
        p = jax.nn.softmax(s.reshape(s.shape[:-2] + (kh * NA_SPAN_W,)), axis=-1).reshape(s.shape)
        o = jnp.einsum('bhjqak,bjakhd->bjqhd', p.astype(v.dtype), v_blk)
        return o.reshape(bsz, GRID_W, B_DIM)

    out = lax.map(row_step, jnp.arange(rows))
    return out.transpose(1, 0, 2, 3).reshape(bsz, t, B_DIM)


def attention_mixer(x, w_in, sink, rpb, w_out):
    bsz, t, _ = x.shape
    h = x @ w_in
    qa, ka, va, qb, kb, vb = jnp.split(
        h, [A_Q_DIM, A_Q_DIM + A_KV_DIM, A_Q_DIM + 2 * A_KV_DIM,
            A_Q_DIM + 2 * A_KV_DIM + B_DIM, A_Q_DIM + 2 * A_KV_DIM + 2 * B_DIM], axis=-1)
    pos = jnp.arange(t)
    qa = rope(qa.reshape(bsz, t, A_Q_HEADS, HEAD_DIM), pos)
    ka = rope(ka.reshape(bsz, t, A_KV_HEADS, HEAD_DIM), pos)
    va = va.reshape(bsz, t, A_KV_HEADS, HEAD_DIM)
    oa = windowed_gqa_sink(qa, ka, va, sink)
    shp = (bsz, t, B_HEADS, HEAD_DIM)
    ob = neighborhood_attention(qb.reshape(shp), kb.reshape(shp), vb.reshape(shp), rpb)
    return jnp.concatenate([oa, ob], axis=-1) @ w_out


def depthwise_conv_centred(u, w, b):
    c = u.shape[-1]
    pad = SSM_CONV // 2
    out = lax.conv_general_dilated(u, w[:, None, :].astype(u.dtype), window_strides=(1,),
                                   padding=[(pad, pad)], dimension_numbers=('NWC', 'WIO', 'NWC'),
                                   feature_group_count=c)
    return out + b


def ssd_chunked(x, dt, a_neg, bm, cm):
    bsz, t = x.shape[:2]
    nc, L, G, R = t // SSM_CHUNK, SSM_CHUNK, SSM_GROUPS, SSM_HEADS // SSM_GROUPS
    x = x.astype(jnp.float32)
    dt = dt.astype(jnp.float32)
    xc = (x * dt[..., None]).reshape(bsz, nc, L, G, R, SSM_HEAD_DIM)
    a = (dt * a_neg.astype(jnp.float32)).reshape(bsz, nc, L, G, R).transpose(0, 1, 3, 4, 2)
    acum = jnp.cumsum(a, axis=-1)
    bc = bm.astype(jnp.float32).reshape(bsz, nc, L, G, SSM_STATE)
    cc = cm.astype(jnp.float32).reshape(bsz, nc, L, G, SSM_STATE)
    tri = jnp.tril(jnp.ones((L, L), dtype=bool))
    lmat = jnp.exp(jnp.where(tri, acum[..., :, None] - acum[..., None, :], -jnp.inf))
    cb = jnp.einsum('bclgn,bcsgn->bcgls', cc, bc)
    y_diag = jnp.einsum('bcgrls,bcsgrp->bclgrp', cb[:, :, :, None] * lmat, xc)
    decay_states = jnp.exp(acum[..., -1:] - acum)
    states = jnp.einsum('bcsgn,bcgrs,bcsgrp->bcgrpn', bc, decay_states, xc)
    chunk_decay = jnp.exp(acum[..., -1])

    def step(h, inp):
        st, dec = inp
        return h * dec[..., None, None] + st, h

    h0 = jnp.zeros((bsz, G, R, SSM_HEAD_DIM, SSM_STATE), jnp.float32)
    _, prev = lax.scan(step, h0, (jnp.moveaxis(states, 1, 0), jnp.moveaxis(chunk_decay, 1, 0)))
    prev = jnp.moveaxis(prev, 0, 1)
    y_off = jnp.einsum('bclgn,bcgrpn,bcgrl->bclgrp', cc, prev, jnp.exp(acum))
    return (y_diag + y_off).reshape(bsz, t, SSM_HEADS, SSM_HEAD_DIM)


def mamba2_bidir_mixer(x, w_in, conv_w, conv_b, dt_bias, a_log, d_skip, norm_w, w_out):
    bsz, t, _ = x.shape
    zxbcdt = x @ w_in
    z, xbc, dt = jnp.split(zxbcdt, [SSM_D_INNER, SSM_D_INNER + SSM_CONV_DIM], axis=-1)
    xbc = jax.nn.silu(depthwise_conv_centred(xbc, conv_w, conv_b))
    xs, bm, cm = jnp.split(xbc, [SSM_D_INNER, SSM_D_INNER + SSM_GROUPS * SSM_STATE], axis=-1)
    xs = xs.reshape(bsz, t, SSM_HEADS, SSM_HEAD_DIM)
    bm = bm.reshape(bsz, t, SSM_GROUPS, SSM_STATE)
    cm = cm.reshape(bsz, t, SSM_GROUPS, SSM_STATE)
    dt = jax.nn.softplus((dt + dt_bias).astype(jnp.float32))
    dt_f, dt_b = dt[..., :SSM_HEADS], dt[..., SSM_HEADS:]
    a_neg = -jnp.exp(a_log.astype(jnp.float32))
    y_f = ssd_chunked(xs, dt_f, a_neg[0], bm, cm)
    flip = lambda u: jnp.flip(u, axis=1)
    y_b = flip(ssd_chunked(flip(xs), flip(dt_b), a_neg[1], flip(bm), flip(cm)))
    y = (y_f + y_b).astype(x.dtype) + xs * d_skip[:, None]
    y = y.reshape(bsz, t, SSM_D_INNER) * jax.nn.silu(z)
    yg = y.astype(jnp.float32).reshape(bsz, t, SSM_GROUPS, SSM_D_INNER // SSM_GROUPS)
    yg = yg * lax.rsqrt(jnp.mean(jnp.square(yg), -1, keepdims=True) + RMS_EPS)
    y = yg.reshape(bsz, t, SSM_D_INNER).astype(x.dtype) * norm_w
    return y @ w_out


def squared_relu_mlp(x, w1, w2):
    return jnp.square(jax.nn.relu(x @ w1)) @ w2


def setup_inputs(seed: int = 0) -> dict:
    key = jax.random.key(seed)
    ks = jax.random.split(key, 20)
    f32 = jnp.float32
    nrm = lambda k, shp, sc: jax.random.normal(k, shp, f32) * sc
    dt0 = jnp.exp(jax.random.uniform(ks[9], (N_SSM_LAYERS, 2 * SSM_HEADS), f32,
                                     jnp.log(1e-3), jnp.log(1e-1)))
    return {
        "x": nrm(ks[0], (BATCH, SEQ, D_MODEL), 1.0),
        "attn_w_in": nrm(ks[1], (N_ATTN_LAYERS, D_MODEL, ATTN_IN), D_MODEL ** -0.5),
        "attn_sink": nrm(ks[2], (N_ATTN_LAYERS, A_Q_HEADS), 1.0),
        "attn_rpb": nrm(ks[3], (N_ATTN_LAYERS, B_HEADS, 2 * NA_MAX_KH - 1, 2 * NA_KW - 1), 0.1),
        "attn_w_out": nrm(ks[4], (N_ATTN_LAYERS, ATTN_OUT, D_MODEL), ATTN_OUT ** -0.5 * DN_BETA),
        "ssm_w_in": nrm(ks[5], (N_SSM_LAYERS, D_MODEL, SSM_IN), D_MODEL ** -0.5),
        "ssm_conv_w": nrm(ks[6], (N_SSM_LAYERS, SSM_CONV, SSM_CONV_DIM), SSM_CONV ** -0.5),
        "ssm_conv_b": nrm(ks[7], (N_SSM_LAYERS, SSM_CONV_DIM), 0.02),
        "ssm_dt_bias": dt0 + jnp.log(-jnp.expm1(-dt0)),
        "ssm_A_log": jnp.log(jax.random.uniform(ks[10], (N_SSM_LAYERS, 2, SSM_HEADS), f32, 1.0, 16.0)),
        "ssm_D": 1.0 + nrm(ks[11], (N_SSM_LAYERS, SSM_HEADS), 0.05),
        "ssm_norm_w": 1.0 + nrm(ks[12], (N_SSM_LAYERS, SSM_D_INNER), 0.05),
        "ssm_w_out": nrm(ks[13], (N_SSM_LAYERS, SSM_D_INNER, D_MODEL), SSM_D_INNER ** -0.5 * DN_BETA),
        "mlp_w1": nrm(ks[14], (DEPTH, D_MODEL, D_FF), D_MODEL ** -0.5),
        "mlp_w2": nrm(ks[15], (DEPTH, D_FF, D_MODEL), D_FF ** -0.5 * DN_BETA),
        "ln1_g": 1.0 + nrm(ks[16], (DEPTH, D_MODEL), 0.05),
        "ln1_b": nrm(ks[17], (DEPTH, D_MODEL), 0.02),
        "ln2_g": 1.0 + nrm(ks[18], (DEPTH, D_MODEL), 0.05),
        "ln2_b": nrm(ks[19], (DEPTH, D_MODEL), 0.02),
    }


def reference(x, attn_w_in, attn_sink, attn_rpb, attn_w_out, ssm_w_in, ssm_conv_w, ssm_conv_b,
              ssm_dt_bias, ssm_A_log, ssm_D, ssm_norm_w, ssm_w_out, mlp_w1, mlp_w2,
              ln1_g, ln1_b, ln2_g, ln2_b):
    for i in range(DEPTH):
        j = i // 2
        if i % 2 == 0:
            mix = attention_mixer(x, attn_w_in[j], attn_sink[j], attn_rpb[j], attn_w_out[j])
        else:
            mix = mamba2_bidir_mixer(x, ssm_w_in[j], ssm_conv_w[j], ssm_conv_b[j], ssm_dt_bias[j],
                                     ssm_A_log[j], ssm_D[j], ssm_norm_w[j], ssm_w_out[j])
        x = layer_norm(DN_ALPHA * x + mix, ln1_g[i], ln1_b[i])
        x = layer_norm(DN_ALPHA * x + squared_relu_mlp(x, mlp_w1[i], mlp_w2[i]), ln2_g[i], ln2_b[i])
    return x
```

```python
import functools

import numpy as np
import jax
import jax.numpy as jnp
from jax import lax
from jax.experimental import pallas as pl
from jax.experimental.pallas import tpu as pltpu

F32 = jnp.float32
BF16 = jnp.bfloat16

D_MODEL = 1024
HEAD_DIM = 64
A_Q_HEADS = 8
A_KV_HEADS = 2
A_BLOCK = 128
ROPE_THETA = 10000.0
B_HEADS = 8
GRID_W = 64
NA_KH = 8
NA_KW = 16
NA_QROWS = 4
NA_KROWS = 12
A_Q_DIM = A_Q_HEADS * HEAD_DIM
A_KV_DIM = A_KV_HEADS * HEAD_DIM
B_DIM = B_HEADS * HEAD_DIM
SSM_D_INNER = 2 * D_MODEL
SSM_HEADS = 32
SSM_GROUPS = 8
SSM_STATE = 128
SSM_CONV = 5
SSM_CHUNK = 128
SSM_GROUP_W = SSM_D_INNER // SSM_GROUPS
SSM_HEADS_PER_GROUP = SSM_HEADS // SSM_GROUPS
SSM_CONV_DIM = SSM_D_INNER + 2 * SSM_GROUPS * SSM_STATE
D_FF = 4 * D_MODEL
DEPTH = 2
DN_ALPHA = (2 * DEPTH) ** 0.25
LN_EPS = 1e-5
RMS_EPS = 1e-5
NEG_INF = -1e30
LANES = 128
HALO = 8

VMEM_LIMIT = 56 * 1024 * 1024


def _params(*sem):
    return pltpu.CompilerParams(dimension_semantics=sem, vmem_limit_bytes=VMEM_LIMIT)


def _resident(shape):
    nd = len(shape)
    return pl.BlockSpec(shape, lambda *_: (0,) * nd, pipeline_mode=pl.Buffered(1))


def _layer_norm(v, g, b):
    mu = jnp.mean(v, axis=-1, keepdims=True)
    d = v - mu
    var = jnp.mean(d * d, axis=-1, keepdims=True)
    return d * lax.rsqrt(var + LN_EPS) * g + b


def _silu(v):
    return v / (1.0 + jnp.exp(-v))


def _attn_in_kernel(x_ref, w_ref, cos_ref, sin_ref, qa_ref, ka_ref, va_ref, qb_ref, kb_ref, vb_ref):
    tm = x_ref.shape[0]
    xb = x_ref[...].astype(BF16)
    cos = cos_ref[...]
    sin = sin_ref[...]
    lane = lax.broadcasted_iota(jnp.int32, (tm, LANES), 1)
    first_half = (lane % HEAD_DIM) < (HEAD_DIM // 2)
    scale = HEAD_DIM ** -0.5

    def rope(h):
        partner = jnp.where(first_half, pltpu.roll(h, LANES - HEAD_DIM // 2, 1),
                            pltpu.roll(h, HEAD_DIM // 2, 1))
        return h * cos + partner * sin

    n_rope = A_Q_DIM + A_KV_DIM
    h = jnp.dot(xb, w_ref[:, :n_rope], preferred_element_type=F32)
    for j in range(A_Q_DIM // LANES):
        qa_ref[:, j * LANES:(j + 1) * LANES] = (rope(h[:, j * LANES:(j + 1) * LANES]) * scale).astype(BF16)
    for j in range(A_KV_DIM // LANES):
        c0 = A_Q_DIM + j * LANES
        ka_ref[:, j * LANES:(j + 1) * LANES] = rope(h[:, c0:c0 + LANES]).astype(BF16)
    c0 = n_rope
    va_ref[...] = jnp.dot(xb, w_ref[:, c0:c0 + A_KV_DIM], preferred_element_type=F32).astype(BF16)
    c0 += A_KV_DIM
    qb_ref[...] = (jnp.dot(xb, w_ref[:, c0:c0 + B_DIM], preferred_element_type=F32) * scale).astype(BF16)
    c0 += B_DIM
    kb_ref[...] = jnp.dot(xb, w_ref[:, c0:c0 + B_DIM], preferred_element_type=F32).astype(BF16)
    c0 += B_DIM
    vb_ref[...] = jnp.dot(xb, w_ref[:, c0:c0 + B_DIM], preferred_element_type=F32).astype(BF16)


def _attn_in_proj(x2d, w_bf16, seq):
    m = x2d.shape[0]
    tm = 512
    half = HEAD_DIM // 2
    inv = ROPE_THETA ** (-jnp.arange(half, dtype=F32) / half)
    ang = jnp.arange(seq, dtype=F32)[:, None] * inv[None, :]
    reps = LANES // half
    cos_t = jnp.tile(jnp.cos(ang), (1, reps))
    sign = jnp.tile(jnp.concatenate([-jnp.ones((half,), F32), jnp.ones((half,), F32)]), LANES // HEAD_DIM)
    sin_t = jnp.tile(jnp.sin(ang), (1, reps)) * sign[None, :]
    tpb = seq // tm
    row = lambda i: (i, 0)
    pos = lambda i: (i % tpb, 0)
    outs = [A_Q_DIM, A_KV_DIM, A_KV_DIM, B_DIM, B_DIM, B_DIM]
    return pl.pallas_call(
        _attn_in_kernel,
        grid=(m // tm,),
        in_specs=[pl.BlockSpec((tm, D_MODEL), row), _resident(w_bf16.shape),
                  pl.BlockSpec((tm, LANES), pos), pl.BlockSpec((tm, LANES), pos)],
        out_specs=[pl.BlockSpec((tm, n), row) for n in outs],
        out_shape=[jax.ShapeDtypeStruct((m, n), BF16) for n in outs],
        compiler_params=_params("parallel"),
        name="attn_in_proj",
    )(x2d, w_bf16, cos_t, sin_t)


def _win_attn_kernel(sink_ref, q_ref, kp_ref, kc_ref, kn_ref, vp_ref, vc_ref, vn_ref, o_ref):
    n = pl.program_id(1)
    nb = pl.num_programs(1)
    blk = A_BLOCK
    rep = A_Q_HEADS // A_KV_HEADS
    qi = lax.broadcasted_iota(jnp.int32, (rep * blk, 3 * blk), 0) % blk
    ki = lax.broadcasted_iota(jnp.int32, (rep * blk, 3 * blk), 1)
    valid = (jnp.abs(qi + blk - ki) <= blk) & ((ki >= blk) | (n > 0)) & ((ki < 2 * blk) | (n < nb - 1))
    k_all = jnp.concatenate([kp_ref[...], kc_ref[...], kn_ref[...]], axis=0)
    v_all = jnp.concatenate([vp_ref[...], vc_ref[...], vn_ref[...]], axis=0)
    for g in range(A_KV_HEADS):
        kg = k_all[:, g * HEAD_DIM:(g + 1) * HEAD_DIM]
        vg = v_all[:, g * HEAD_DIM:(g + 1) * HEAD_DIM]
        heads = [g * rep + r for r in range(rep)]
        q4 = jnp.concatenate([q_ref[:, h * HEAD_DIM:(h + 1) * HEAD_DIM] for h in heads], axis=0)
        sink = jnp.concatenate([jnp.full((blk, 1), sink_ref[h], F32) for h in heads], axis=0)
        s = lax.dot_general(q4, kg, (((1,), (1,)), ((), ())), preferred_element_type=F32)
        s = jnp.where(valid, s, NEG_INF)
        mx = jnp.maximum(jnp.max(s, axis=-1, keepdims=True), sink)
        p = jnp.exp(s - mx)
        den = jnp.sum(p, axis=-1, keepdims=True) + jnp.exp(sink - mx)
        o = jnp.dot(p.astype(BF16), vg, preferred_element_type=F32) / den
        for r, h in enumerate(heads):
            o_ref[:, h * HEAD_DIM:(h + 1) * HEAD_DIM] = o[r * blk:(r + 1) * blk].astype(BF16)


def _win_attn(qa, ka, va, sink, bsz, seq):
    nb = seq // A_BLOCK
    cur = lambda b, n: (b * nb + n, 0)
    prv = lambda b, n: (b * nb + jnp.maximum(n - 1, 0), 0)
    nxt = lambda b, n: (b * nb + jnp.minimum(n + 1, nb - 1), 0)
    kv = lambda im: pl.BlockSpec((A_BLOCK, A_KV_DIM), im)
    return pl.pallas_call(
        _win_attn_kernel,
        grid=(bsz, nb),
        in_specs=[pl.BlockSpec(memory_space=pltpu.SMEM),
                  pl.BlockSpec((A_BLOCK, A_Q_DIM), cur),
                  kv(prv), kv(cur), kv(nxt), kv(prv), kv(cur), kv(nxt)],
        out_specs=pl.BlockSpec((A_BLOCK, A_Q_DIM), cur),
        out_shape=jax.ShapeDtypeStruct(qa.shape, BF16),
        compiler_params=_params("parallel", "parallel"),
        name="window_attn",
    )(sink.astype(F32), qa, ka, ka, ka, va, va, va)


def _na_tables(rows):
    nj = rows // NA_QROWS
    assert rows >= NA_KROWS + NA_QROWS and rows % NA_QROWS == 0

    def tile(j):
        ws = min(max(NA_QROWS * j - NA_KH // 2, 0), rows - NA_KROWS)
        rq = np.arange(NA_QROWS)[:, None, None, None]
        cq = np.arange(GRID_W)[None, :, None, None]
        rk = np.arange(NA_KROWS)[None, None, :, None]
        ck = np.arange(GRID_W)[None, None, None, :]
        r = NA_QROWS * j + rq
        rs = np.clip(r - NA_KH // 2, 0, rows - NA_KH)
        cs = np.clip(cq - NA_KW // 2, 0, GRID_W - NA_KW)
        ok = (ws + rk >= rs) & (ws + rk < rs + NA_KH) & (ck >= cs) & (ck < cs + NA_KW)
        dr = np.clip(ws + rk - r + NA_KH - 1, 0, 2 * NA_KH - 2)
        dc = np.clip(ck - cq + NA_KW - 1, 0, 2 * NA_KW - 2)
        shp = (NA_QROWS * GRID_W, NA_KROWS * GRID_W)
        full = np.broadcast_to
        return (full(ok, ok.shape[:0] + (NA_QROWS, GRID_W, NA_KROWS, GRID_W)).reshape(shp),
                full(dr, (NA_QROWS, GRID_W, NA_KROWS, GRID_W)).reshape(shp),
                full(dc, (NA_QROWS, GRID_W, NA_KROWS, GRID_W)).reshape(shp))

    cases = [tile(0), tile(1), tile(nj - 1)]
    for j in range(1, nj - 1):
        t = tile(j)
        assert all(np.array_equal(a, b) for a, b in zip(t, cases[1]))
    ok = np.stack([c[0] for c in cases])
    dr = np.stack([c[1] for c in cases])
    dc = np.stack([c[2] for c in cases])
    return ok, dr, dc


def _na_kernel(q_ref, k0_ref, k1_ref, k2_ref, v0_ref, v1_ref, v2_ref, bias_ref, o_ref):
    k_all = jnp.concatenate([k0_ref[...], k1_ref[...], k2_ref[...]], axis=0)
    v_all = jnp.concatenate([v0_ref[...], v1_ref[...], v2_ref[...]], axis=0)
    for h in range(B_HEADS):
        sl = slice(h * HEAD_DIM, (h + 1) * HEAD_DIM)
        s = lax.dot_general(q_ref[:, sl], k_all[:, sl], (((1,), (1,)), ((), ())),
                            preferred_element_type=F32)
        s = s + bias_ref[0, h]
        mx = jnp.max(s, axis=-1, keepdims=True)
        p = jnp.exp(s - mx)
        den = jnp.sum(p, axis=-1, keepdims=True)
        o = jnp.dot(p.astype(BF16), v_all[:, sl], preferred_element_type=F32) / den
        o_ref[:, sl] = o.astype(BF16)


def _na_attn(qb, kb, vb, rpb, bsz, seq):
    rows = seq // GRID_W
    nj = rows // NA_QROWS
    ok, dr, dc = _na_tables(rows)
    bias = jnp.where(ok[:, None], rpb.astype(F32)[:, dr, dc].transpose(1, 0, 2, 3), NEG_INF)
    tq = NA_QROWS * GRID_W
    kblocks = NA_KROWS // NA_QROWS
    per_b = seq // tq
    qmap = lambda b, j: (b * per_b + j, 0)

    def kmap(i):
        return lambda b, j: (b * per_b + jnp.clip(j - 1, 0, nj - kblocks) + i, 0)

    case = lambda b, j: (jnp.where(j == 0, 0, jnp.where(j == nj - 1, 2, 1)), 0, 0, 0)
    kv = [pl.BlockSpec((tq, B_DIM), kmap(i)) for i in range(kblocks)]
    return pl.pallas_call(
        _na_kernel,
        grid=(bsz, nj),
        in_specs=[pl.BlockSpec((tq, B_DIM), qmap)] + kv + kv
                 + [pl.BlockSpec((1, B_HEADS, tq, NA_KROWS * GRID_W), case)],
        out_specs=pl.BlockSpec((tq, B_DIM), qmap),
        out_shape=jax.ShapeDtypeStruct(qb.shape, BF16),
        compiler_params=_params("parallel", "arbitrary"),
        name="neighborhood_attn",
    )(qb, kb, kb, kb, vb, vb, vb, bias)


def _post_ln_mlp(x, mix, g1, b1, w1_ref, w2_ref, g2, b2):
    x1 = _layer_norm(DN_ALPHA * x + mix, g1, b1)
    xb = x1.astype(BF16)
    acc = jnp.zeros_like(x1)
    fc = 1024
    for c in range(D_FF // fc):
        h = jnp.dot(xb, w1_ref[:, c * fc:(c + 1) * fc], preferred_element_type=F32)
        h = jnp.maximum(h, 0.0)
        acc = acc + jnp.dot((h * h).astype(BF16), w2_ref[c * fc:(c + 1) * fc, :],
                            preferred_element_type=F32)
    return _layer_norm(DN_ALPHA * x1 + acc, g2, b2)


def _attn_tail_kernel(oa_ref, ob_ref, x_ref, wo_ref, g1_ref, b1_ref, w1_ref, w2_ref, g2_ref, b2_ref,
                      out_ref):
    mix = (jnp.dot(oa_ref[...], wo_ref[:A_Q_DIM, :], preferred_element_type=F32)
           + jnp.dot(ob_ref[...], wo_ref[A_Q_DIM:, :], preferred_element_type=F32))
    out_ref[...] = _post_ln_mlp(x_ref[...], mix, g1_ref[...], b1_ref[...], w1_ref, w2_ref,
                                g2_ref[...], b2_ref[...])


def _attn_tail(oa, ob, x2d, wo, g1, b1, w1, w2, g2, b2):
    m = x2d.shape[0]
    tm = 512
    row = lambda i: (i, 0)
    vec = lambda a: a.reshape(1, -1).astype(F32)
    return pl.pallas_call(
        _attn_tail_kernel,
        grid=(m // tm,),
        in_specs=[pl.BlockSpec((tm, A_Q_DIM), row), pl.BlockSpec((tm, B_DIM), row),
                  pl.BlockSpec((tm, D_MODEL), row), _resident(wo.shape),
                  _resident((1, D_MODEL)), _resident((1, D_MODEL)),
                  _resident(w1.shape), _resident(w2.shape),
                  _resident((1, D_MODEL)), _resident((1, D_MODEL))],
        out_specs=pl.BlockSpec((tm, D_MODEL), row),
        out_shape=jax.ShapeDtypeStruct((m, D_MODEL), F32),
        compiler_params=_params("parallel"),
        name="attn_out_mlp",
    )(oa, ob, x2d, wo, vec(g1), vec(b1), w1, w2, vec(g2), vec(b2))


def _ssm_tail_kernel(yf_ref, yb_ref, xs_ref, z_ref, dskip_ref, nw_ref, x_ref, wo_ref, g1_ref, b1_ref,
                     w1_ref, w2_ref, g2_ref, b2_ref, out_ref):
    y = yf_ref[...].astype(F32) + yb_ref[...].astype(F32) + xs_ref[...].astype(F32) * dskip_ref[...]
    y = y * _silu(z_ref[...].astype(F32))
    parts = []
    for g in range(SSM_GROUPS):
        yg = y[:, g * SSM_GROUP_W:(g + 1) * SSM_GROUP_W]
        ms = jnp.mean(yg * yg, axis=-1, keepdims=True)
        parts.append(yg * lax.rsqrt(ms + RMS_EPS))
    yn = (jnp.concatenate(parts, axis=-1) * nw_ref[...]).astype(BF16)
    mix = jnp.dot(yn, wo_ref[...], preferred_element_type=F32)
    out_ref[...] = _post_ln_mlp(x_ref[...], mix, g1_ref[...], b1_ref[...], w1_ref, w2_ref,
                                g2_ref[...], b2_ref[...])


def _ssm_tail(yf, yb, xbc, z, d_skip, norm_w, x2d, wo, g1, b1, w1, w2, g2, b2):
    m = x2d.shape[0]
    tm = 512
    row = lambda i: (i, 0)
    vec = lambda a: a.reshape(1, -1).astype(F32)
    wide = pl.BlockSpec((tm, SSM_D_INNER), row)
    dskip_e = jnp.repeat(d_skip.astype(F32), SSM_D_INNER // SSM_HEADS).reshape(1, -1)
    return pl.pallas_call(
        _ssm_tail_kernel,
        grid=(m // tm,),
        in_specs=[wide, wide, wide, wide, _resident((1, SSM_D_INNER)), _resident((1, SSM_D_INNER)),
                  pl.BlockSpec((tm, D_MODEL), row), _resident(wo.shape),
                  _resident((1, D_MODEL)), _resident((1, D_MODEL)),
                  _resident(w1.shape), _resident(w2.shape),
                  _resident((1, D_MODEL)), _resident((1, D_MODEL))],
        out_specs=pl.BlockSpec((tm, D_MODEL), row),
        out_shape=jax.ShapeDtypeStruct((m, D_MODEL), F32),
        compiler_params=_params("parallel"),
        name="ssm_out_mlp",
    )(yf, yb, xbc, z, dskip_e, vec(norm_w), x2d, wo, vec(g1), vec(b1), w1, w2, vec(g2), vec(b2))


def _ssm_in_kernel(x_ref, w_ref, z_ref, xbc_ref, dt_ref):
    xb = x_ref[...].astype(BF16)
    nc = 1024
    for c in range(SSM_D_INNER // nc):
        z_ref[:, c * nc:(c + 1) * nc] = jnp.dot(
            xb, w_ref[:, c * nc:(c + 1) * nc], preferred_element_type=F32).astype(BF16)
    for c in range(SSM_CONV_DIM // nc):
        c0 = SSM_D_INNER + c * nc
        xbc_ref[:, c * nc:(c + 1) * nc] = jnp.dot(xb, w_ref[:, c0:c0 + nc], preferred_element_type=F32)
    c0 = SSM_D_INNER + SSM_CONV_DIM
    dt_ref[...] = jnp.dot(xb, w_ref[:, c0:c0 + LANES], preferred_element_type=F32)


def _ssm_in_proj(x2d, w_pad):
    m = x2d.shape[0]
    tm = 512
    row = lambda i: (i, 0)
    return pl.pallas_call(
        _ssm_in_kernel,
        grid=(m // tm,),
        in_specs=[pl.BlockSpec((tm, D_MODEL), row), _resident(w_pad.shape)],
        out_specs=[pl.BlockSpec((tm, SSM_D_INNER), row), pl.BlockSpec((tm, SSM_CONV_DIM), row),
                   pl.BlockSpec((tm, LANES), row)],
        out_shape=[jax.ShapeDtypeStruct((m, SSM_D_INNER), BF16),
                   jax.ShapeDtypeStruct((m, SSM_CONV_DIM), F32),
                   jax.ShapeDtypeStruct((m, LANES), F32)],
        compiler_params=_params("parallel"),
        name="ssm_in_proj",
    )(x2d, w_pad)


def _conv_kernel(tiles_per_seq, prev_ref, main_ref, next_ref, w_ref, b_ref, o_ref):
    i = pl.program_id(0)
    tm = main_ref.shape[0]
    pos = i % tiles_per_seq
    prev = jnp.where(pos > 0, prev_ref[...], 0.0)
    nxt = jnp.where(pos < tiles_per_seq - 1, next_ref[...], 0.0)
    ext = jnp.concatenate([prev, main_ref[...], nxt], axis=0)
    n_ext = tm + 2 * HALO
    pad = SSM_CONV // 2
    acc = jnp.zeros(main_ref.shape, F32) + b_ref[...]
    for k in range(SSM_CONV):
        shift = (pad - k) % n_ext
        tap = ext if shift == 0 else pltpu.roll(ext, shift, 0)
        acc = acc + tap[HALO:HALO + tm] * w_ref[k:k + 1, :]
    o_ref[...] = _silu(acc).astype(BF16)


def _conv_silu(xbc_raw, conv_w, conv_b, seq):
    m, n = xbc_raw.shape
    tm, nc = 512, 1024
    tpb = seq // tm
    hb = tm // HALO
    last = m // HALO - 1
    return pl.pallas_call(
        functools.partial(_conv_kernel, tpb),
        grid=(m // tm, n // nc),
        in_specs=[pl.BlockSpec((HALO, nc), lambda i, j: (jnp.maximum(i * hb - 1, 0), j)),
                  pl.BlockSpec((tm, nc), lambda i, j: (i, j)),
                  pl.BlockSpec((HALO, nc), lambda i, j: (jnp.minimum((i + 1) * hb, last), j)),
                  pl.BlockSpec((SSM_CONV, nc), lambda i, j: (0, j)),
                  pl.BlockSpec((1, nc), lambda i, j: (0, j))],
        out_specs=pl.BlockSpec((tm, nc), lambda i, j: (i, j)),
        out_shape=jax.ShapeDtypeStruct((m, n), BF16),
        compiler_params=_params("parallel", "parallel"),
        name="conv_silu",
    )(xbc_raw, xbc_raw, xbc_raw, conv_w.astype(F32), conv_b.reshape(1, -1).astype(F32))


def _ssd_kernel(reverse, xs_ref, b_ref, c_ref, dtraw_ref, dtb_ref, alog_ref, y_ref, state_ref, act_ref):
    L = SSM_CHUNK
    hpg = SSM_HEADS_PER_GROUP
    off = SSM_HEADS if reverse else 0

    @pl.when(pl.program_id(1) == 0)
    def _():
        state_ref[...] = jnp.zeros_like(state_ref)

    v = dtraw_ref[...] + dtb_ref[...]
    dt = jnp.maximum(v, 0.0) + jnp.log1p(jnp.exp(-jnp.abs(v)))
    a = dt * (-jnp.exp(alog_ref[...]))
    li = lax.broadcasted_iota(jnp.int32, (L, L), 0)
    si = lax.broadcasted_iota(jnp.int32, (L, L), 1)
    ac = a
    s = 1
    while s < L:
        if reverse:
            ac = ac + jnp.where(li < L - s, pltpu.roll(ac, L - s, 0), 0.0)
        else:
            ac = ac + jnp.where(li >= s, pltpu.roll(ac, s, 0), 0.0)
        s *= 2
    act_ref[...] = ac.T
    tot = ac[0:1, :] if reverse else ac[L - 1:L, :]
    tri = (si >= li) if reverse else (li >= si)
    head_of_lane = lax.broadcasted_iota(jnp.int32, (L, SSM_GROUP_W), 1) // (SSM_GROUP_W // hpg)

    def expand(col_src, g):
        w = SSM_GROUP_W // hpg
        return jnp.concatenate(
            [jnp.broadcast_to(col_src[:, off + g * hpg + r: off + g * hpg + r + 1], (col_src.shape[0], w))
             for r in range(hpg)], axis=1)

    for g in range(SSM_GROUPS):
        bg = b_ref[:, g * SSM_STATE:(g + 1) * SSM_STATE]
        cg = c_ref[:, g * SSM_STATE:(g + 1) * SSM_STATE]
        xg = xs_ref[:, g * SSM_GROUP_W:(g + 1) * SSM_GROUP_W].astype(F32)
        cb = lax.dot_general(cg, bg, (((1,), (1,)), ((), ())), preferred_element_type=F32)
        ac_e = expand(ac, g)
        tot_e = expand(tot, g)
        xc = xg * expand(dt, g)
        xcb = xc.astype(BF16)
        prev = state_ref[g]
        y_off = jnp.dot(cg, prev.astype(BF16), preferred_element_type=F32) * jnp.exp(ac_e)
        ms = []
        for r in range(hpg):
            h = off + g * hpg + r
            col = jnp.broadcast_to(ac[:, h:h + 1], (L, L))
            rowv = act_ref[h:h + 1, :]
            lm = jnp.exp(jnp.where(tri, col - rowv, NEG_INF))
            ms.append((cb * lm).astype(BF16))
        lhs = jnp.concatenate(ms, axis=1)
        rhs = jnp.concatenate([jnp.where(head_of_lane == r, xcb, jnp.zeros_like(xcb)) for r in range(hpg)],
                              axis=0)
        y_diag = jnp.dot(lhs, rhs, preferred_element_type=F32)
        y_ref[:, g * SSM_GROUP_W:(g + 1) * SSM_GROUP_W] = (y_diag + y_off).astype(y_ref.dtype)
        xd = (xc * jnp.exp(tot_e - ac_e)).astype(BF16)
        contrib = lax.dot_general(bg, xd, (((0,), (0,)), ((), ())), preferred_element_type=F32)
        state_ref[g] = prev * jnp.exp(tot_e) + contrib


def _ssd_scan(xbc, dt_raw, dt_bias_row, alog_row, bsz, seq, reverse):
    m = xbc.shape[0]
    nc = seq // SSM_CHUNK
    L = SSM_CHUNK
    n_bc = SSM_GROUPS * SSM_STATE

    def chunk(b, c):
        return b * nc + (nc - 1 - c if reverse else c)

    return pl.pallas_call(
        functools.partial(_ssd_kernel, reverse),
        grid=(bsz, nc),
        in_specs=[pl.BlockSpec((L, SSM_D_INNER), lambda b, c: (chunk(b, c), 0)),
                  pl.BlockSpec((L, n_bc), lambda b, c: (chunk(b, c), SSM_D_INNER // n_bc)),
                  pl.BlockSpec((L, n_bc), lambda b, c: (chunk(b, c), SSM_D_INNER // n_bc + 1)),
                  pl.BlockSpec((L, LANES), lambda b, c: (chunk(b, c), 0)),
                  _resident((1, LANES)), _resident((1, LANES))],
        out_specs=pl.BlockSpec((L, SSM_D_INNER), lambda b, c: (chunk(b, c), 0)),
        out_shape=jax.ShapeDtypeStruct((m, SSM_D_INNER), BF16),
        scratch_shapes=[pltpu.VMEM((SSM_GROUPS, SSM_STATE, SSM_GROUP_W), F32),
                        pltpu.VMEM((LANES, L), F32)],
        compiler_params=_params("parallel", "arbitrary"),
        name="ssd_bwd" if reverse else "ssd_fwd",
    )(xbc, xbc, xbc, dt_raw, dt_bias_row, alog_row)


def kernel(x, attn_w_in, attn_sink, attn_rpb, attn_w_out, ssm_w_in, ssm_conv_w, ssm_conv_b, ssm_dt_bias,
           ssm_A_log, ssm_D, ssm_norm_w, ssm_w_out, mlp_w1, mlp_w2, ln1_g, ln1_b, ln2_g, ln2_b):
    bsz, seq, _ = x.shape
    x2d = x.reshape(bsz * seq, D_MODEL)

    qa, ka, va, qb, kb, vb = _attn_in_proj(x2d, attn_w_in[0].astype(BF16), seq)
    oa = _win_attn(qa, ka, va, attn_sink[0], bsz, seq)
    ob = _na_attn(qb, kb, vb, attn_rpb[0], bsz, seq)
    x2d = _attn_tail(oa, ob, x2d, attn_w_out[0].astype(BF16), ln1_g[0], ln1_b[0],
                     mlp_w1[0].astype(BF16), mlp_w2[0].astype(BF16), ln2_g[0], ln2_b[0])

    n_dt = 2 * SSM_HEADS
    w_pad = jnp.pad(ssm_w_in[0].astype(BF16), ((0, 0), (0, LANES - n_dt)))
    z, xbc_raw, dt_raw = _ssm_in_proj(x2d, w_pad)
    xbc = _conv_silu(xbc_raw, ssm_conv_w[0], ssm_conv_b[0], seq)
    lane_pad = lambda a: jnp.pad(a.reshape(1, n_dt).astype(F32), ((0, 0), (0, LANES - n_dt)))
    dt_bias_row = lane_pad(ssm_dt_bias[0])
    alog_row = lane_pad(ssm_A_log[0])
    y_b = _ssd_scan(xbc, dt_raw, dt_bias_row, alog_row, bsz, seq, reverse=True)
    y_f = _ssd_scan(xbc, dt_raw, dt_bias_row, alog_row, bsz, seq, reverse=False)
    x2d = _ssm_tail(y_f, y_b, xbc, z, ssm_D[0], ssm_norm_w[0], x2d, ssm_w_out[0].astype(BF16),
                    ln1_g[1], ln1_b[1], mlp_w1[1].astype(BF16), mlp_w2[1].astype(BF16),
                    ln2_g[1], ln2_b[1])
    return x2d.reshape(bsz, seq, D_MODEL)
```

```python
import functools

import numpy as np
import jax
import jax.numpy as jnp
from jax import lax
from jax.experimental import pallas as pl
from jax.experimental.pallas import tpu as pltpu

F32 = jnp.float32
BF16 = jnp.bfloat16

D_MODEL = 1024
HEAD_DIM = 64
A_Q_HEADS = 8
A_KV_HEADS = 2
A_BLOCK = 128
WIN_QBLOCKS = 4
ROPE_THETA = 10000.0
B_HEADS = 8
GRID_W = 64
NA_KH = 8
NA_KW = 16
NA_QROWS = 4
NA_KROWS = 12
A_Q_DIM = A_Q_HEADS * HEAD_DIM
A_KV_DIM = A_KV_HEADS * HEAD_DIM
B_DIM = B_HEADS * HEAD_DIM
SSM_D_INNER = 2 * D_MODEL
SSM_HEADS = 32
SSM_GROUPS = 8
SSM_STATE = 128
SSM_CONV = 5
SSM_CHUNK = 128
SSM_GROUP_W = SSM_D_INNER // SSM_GROUPS
SSM_HEADS_PER_GROUP = SSM_HEADS // SSM_GROUPS
SSM_CONV_DIM = SSM_D_INNER + 2 * SSM_GROUPS * SSM_STATE
D_FF = 4 * D_MODEL
DEPTH = 2
DN_ALPHA = (2 * DEPTH) ** 0.25
LN_EPS = 1e-5
RMS_EPS = 1e-5
NEG_INF = -1e30
LANES = 128
HALO = 8

VMEM_LIMIT = 56 * 1024 * 1024


def _params(*sem):
    return pltpu.CompilerParams(dimension_semantics=sem, vmem_limit_bytes=VMEM_LIMIT)


def _resident(shape):
    nd = len(shape)
    return pl.BlockSpec(shape, lambda *_: (0,) * nd, pipeline_mode=pl.Buffered(1))


def _layer_norm(v, g, b):
    mu = jnp.mean(v, axis=-1, keepdims=True)
    d = v - mu
    var = jnp.mean(d * d, axis=-1, keepdims=True)
    return d * lax.rsqrt(var + LN_EPS) * g + b


def _silu(v):
    return v / (1.0 + jnp.exp(-v))


def _attn_in_kernel(x_ref, w_ref, cos_ref, sin_ref, qa_ref, ka_ref, va_ref, qb_ref, kb_ref, vb_ref):
    tm = x_ref.shape[0]
    xb = x_ref[...].astype(BF16)
    cos = cos_ref[...]
    sin = sin_ref[...]
    lane = lax.broadcasted_iota(jnp.int32, (tm, LANES), 1)
    first_half = (lane % HEAD_DIM) < (HEAD_DIM // 2)
    scale = HEAD_DIM ** -0.5

    def rope(h):
        partner = jnp.where(first_half, pltpu.roll(h, LANES - HEAD_DIM // 2, 1),
                            pltpu.roll(h, HEAD_DIM // 2, 1))
        return h * cos + partner * sin

    n_rope = A_Q_DIM + A_KV_DIM
    h = jnp.dot(xb, w_ref[:, :n_rope], preferred_element_type=F32)
    for j in range(A_Q_DIM // LANES):
        qa_ref[:, j * LANES:(j + 1) * LANES] = (rope(h[:, j * LANES:(j + 1) * LANES]) * scale).astype(BF16)
    for j in range(A_KV_DIM // LANES):
        c0 = A_Q_DIM + j * LANES
        ka_ref[:, j * LANES:(j + 1) * LANES] = rope(h[:, c0:c0 + LANES]).astype(BF16)
    c0 = n_rope
    va_ref[...] = jnp.dot(xb, w_ref[:, c0:c0 + A_KV_DIM], preferred_element_type=F32).astype(BF16)
    c0 += A_KV_DIM
    qb_ref[...] = (jnp.dot(xb, w_ref[:, c0:c0 + B_DIM], preferred_element_type=F32) * scale).astype(BF16)
    c0 += B_DIM
    kb_ref[...] = jnp.dot(xb, w_ref[:, c0:c0 + B_DIM], preferred_element_type=F32).astype(BF16)
    c0 += B_DIM
    vb_ref[...] = jnp.dot(xb, w_ref[:, c0:c0 + B_DIM], preferred_element_type=F32).astype(BF16)


def _attn_in_proj(x2d, w_bf16, seq):
    m = x2d.shape[0]
    tm = 512
    half = HEAD_DIM // 2
    inv = ROPE_THETA ** (-jnp.arange(half, dtype=F32) / half)
    ang = jnp.arange(seq, dtype=F32)[:, None] * inv[None, :]
    reps = LANES // half
    cos_t = jnp.tile(jnp.cos(ang), (1, reps))
    sign = jnp.tile(jnp.concatenate([-jnp.ones((half,), F32), jnp.ones((half,), F32)]), LANES // HEAD_DIM)
    sin_t = jnp.tile(jnp.sin(ang), (1, reps)) * sign[None, :]
    tpb = seq // tm
    row = lambda i: (i, 0)
    pos = lambda i: (i % tpb, 0)
    outs = [A_Q_DIM, A_KV_DIM, A_KV_DIM, B_DIM, B_DIM, B_DIM]
    return pl.pallas_call(
        _attn_in_kernel,
        grid=(m // tm,),
        in_specs=[pl.BlockSpec((tm, D_MODEL), row), _resident(w_bf16.shape),
                  pl.BlockSpec((tm, LANES), pos), pl.BlockSpec((tm, LANES), pos)],
        out_specs=[pl.BlockSpec((tm, n), row) for n in outs],
        out_shape=[jax.ShapeDtypeStruct((m, n), BF16) for n in outs],
        compiler_params=_params("parallel"),
        name="attn_in_proj",
    )(x2d, w_bf16, cos_t, sin_t)


def _win_attn_kernel(sink_ref, q_ref, kp_ref, kc_ref, kn_ref, vp_ref, vc_ref, vn_ref, o_ref):
    i = pl.program_id(1)
    ni = pl.num_programs(1)
    blk = A_BLOCK
    rep = A_Q_HEADS // A_KV_HEADS
    qi = lax.broadcasted_iota(jnp.int32, (rep * blk, 3 * blk), 0) % blk
    ki = lax.broadcasted_iota(jnp.int32, (rep * blk, 3 * blk), 1)
    band = jnp.abs(qi + blk - ki) <= blk
    k_all = jnp.concatenate([kp_ref[...], kc_ref[...], kn_ref[...]], axis=0)
    v_all = jnp.concatenate([vp_ref[...], vc_ref[...], vn_ref[...]], axis=0)
    for sb in range(WIN_QBLOCKS):
        valid = band
        if sb == 0:
            valid = valid & ((ki >= blk) | (i > 0))
        if sb == WIN_QBLOCKS - 1:
            valid = valid & ((ki < 2 * blk) | (i < ni - 1))
        rows = slice(sb * blk, (sb + 1) * blk)
        for g in range(A_KV_HEADS):
            kg = k_all[sb * blk:(sb + 3) * blk, g * HEAD_DIM:(g + 1) * HEAD_DIM]
            vg = v_all[sb * blk:(sb + 3) * blk, g * HEAD_DIM:(g + 1) * HEAD_DIM]
            heads = [g * rep + r for r in range(rep)]
            q4 = jnp.concatenate([q_ref[rows, h * HEAD_DIM:(h + 1) * HEAD_DIM] for h in heads], axis=0)
            s = lax.dot_general(q4, kg, (((1,), (1,)), ((), ())), preferred_element_type=F32)
            s = jnp.where(valid, s, NEG_INF)
            for r, h in enumerate(heads):
                sr = s[r * blk:(r + 1) * blk]
                sink = sink_ref[h]
                mx = jnp.maximum(jnp.max(sr, axis=-1, keepdims=True), sink)
                p = jnp.exp(sr - mx)
                den = jnp.sum(p, axis=-1, keepdims=True) + jnp.exp(sink - mx)
                o = jnp.dot(p.astype(BF16), vg, preferred_element_type=F32) / den
                o_ref[rows, h * HEAD_DIM:(h + 1) * HEAD_DIM] = o.astype(BF16)


def _win_attn(qa, ka, va, sink, bsz, seq):
    nb = seq // A_BLOCK
    tq = WIN_QBLOCKS * A_BLOCK
    ni = seq // tq
    cur = lambda b, i: (b * ni + i, 0)
    prv = lambda b, i: (b * nb + jnp.maximum(i * WIN_QBLOCKS - 1, 0), 0)
    nxt = lambda b, i: (b * nb + jnp.minimum((i + 1) * WIN_QBLOCKS, nb - 1), 0)
    edge = lambda im: pl.BlockSpec((A_BLOCK, A_KV_DIM), im)
    body = pl.BlockSpec((tq, A_KV_DIM), cur)
    return pl.pallas_call(
        _win_attn_kernel,
        grid=(bsz, ni),
        in_specs=[pl.BlockSpec(memory_space=pltpu.SMEM),
                  pl.BlockSpec((tq, A_Q_DIM), cur),
                  edge(prv), body, edge(nxt), edge(prv), body, edge(nxt)],
        out_specs=pl.BlockSpec((tq, A_Q_DIM), cur),
        out_shape=jax.ShapeDtypeStruct(qa.shape, BF16),
        compiler_params=_params("parallel", "parallel"),
        name="window_attn",
    )(sink.astype(F32), qa, ka, ka, ka, va, va, va)


def _na_row_tables(rows):
    nj = rows // NA_QROWS
    assert rows >= NA_KROWS + NA_QROWS and rows % NA_QROWS == 0

    def tile(j):
        ws = min(max(NA_QROWS * j - NA_KH // 2, 0), rows - NA_KROWS)
        r = NA_QROWS * j + np.arange(NA_QROWS)[:, None]
        rk = ws + np.arange(NA_KROWS)[None, :]
        rs = np.clip(r - NA_KH // 2, 0, rows - NA_KH)
        return (rk >= rs) & (rk < rs + NA_KH), rk - r + NA_KH - 1

    cases = [tile(0), tile(1), tile(nj - 1)]
    for j in range(1, nj - 1):
        ok, dr = tile(j)
        assert np.array_equal(ok, cases[1][0]) and np.array_equal(np.where(ok, dr, 0), np.where(ok, cases[1][1], 0))
    return np.stack([c[0] for c in cases]), np.stack([c[1] for c in cases])


def _na_bias(rpb, rows):
    ok_r, dr = _na_row_tables(rows)
    w = GRID_W
    pad = w - NA_KW
    padded = jnp.pad(rpb.astype(F32), ((0, 0), (0, 0), (pad, pad)))
    toep = jnp.stack([padded[:, :, w - 1 - cq:2 * w - 1 - cq] for cq in range(w)], axis=2)
    cq = np.arange(w)[:, None]
    ck = np.arange(w)[None, :]
    cs = np.clip(cq - NA_KW // 2, 0, w - NA_KW)
    toep = jnp.where((ck >= cs) & (ck < cs + NA_KW), toep, NEG_INF)
    outside = jnp.full((B_HEADS, w, w), NEG_INF, F32)
    tiles = []
    for case in range(ok_r.shape[0]):
        q_rows = []
        for rq in range(NA_QROWS):
            blocks = [toep[:, int(dr[case, rq, rk])] if ok_r[case, rq, rk] else outside
                      for rk in range(NA_KROWS)]
            q_rows.append(jnp.concatenate(blocks, axis=-1))
        tiles.append(jnp.concatenate(q_rows, axis=1))
    return jnp.stack(tiles)


def _na_kernel(q_ref, k0_ref, k1_ref, k2_ref, v0_ref, v1_ref, v2_ref, bias_ref, o_ref):
    k_all = jnp.concatenate([k0_ref[...], k1_ref[...], k2_ref[...]], axis=0)
    v_all = jnp.concatenate([v0_ref[...], v1_ref[...], v2_ref[...]], axis=0)
    for h in range(B_HEADS):
        sl = slice(h * HEAD_DIM, (h + 1) * HEAD_DIM)
        s = lax.dot_general(q_ref[:, sl], k_all[:, sl], (((1,), (1,)), ((), ())),
                            preferred_element_type=F32)
        s = s + bias_ref[0, h]
        mx = jnp.max(s, axis=-1, keepdims=True)
        p = jnp.exp(s - mx)
        den = jnp.sum(p, axis=-1, keepdims=True)
        o = jnp.dot(p.astype(BF16), v_all[:, sl], preferred_element_type=F32) / den
        o_ref[:, sl] = o.astype(BF16)


def _na_attn(qb, kb, vb, rpb, bsz, seq):
    rows = seq // GRID_W
    nj = rows // NA_QROWS
    bias = _na_bias(rpb, rows)
    tq = NA_QROWS * GRID_W
    kblocks = NA_KROWS // NA_QROWS
    per_b = seq // tq
    qmap = lambda b, j: (b * per_b + j, 0)

    def kmap(i):
        return lambda b, j: (b * per_b + jnp.clip(j - 1, 0, nj - kblocks) + i, 0)

    case = lambda b, j: (jnp.where(j == 0, 0, jnp.where(j == nj - 1, 2, 1)), 0, 0, 0)
    kv = [pl.BlockSpec((tq, B_DIM), kmap(i)) for i in range(kblocks)]
    return pl.pallas_call(
        _na_kernel,
        grid=(bsz, nj),
        in_specs=[pl.BlockSpec((tq, B_DIM), qmap)] + kv + kv
                 + [pl.BlockSpec((1, B_HEADS, tq, NA_KROWS * GRID_W), case)],
        out_specs=pl.BlockSpec((tq, B_DIM), qmap),
        out_shape=jax.ShapeDtypeStruct(qb.shape, BF16),
        compiler_params=_params("parallel", "arbitrary"),
        name="neighborhood_attn",
    )(qb, kb, kb, kb, vb, vb, vb, bias)


def _post_ln_mlp(x, mix, g1, b1, w1_ref, w2_ref, g2, b2):
    x1 = _layer_norm(DN_ALPHA * x + mix, g1, b1)
    xb = x1.astype(BF16)
    acc = jnp.zeros_like(x1)
    fc = 1024
    for c in range(D_FF // fc):
        h = jnp.dot(xb, w1_ref[:, c * fc:(c + 1) * fc], preferred_element_type=F32)
        h = jnp.maximum(h, 0.0)
        acc = acc + jnp.dot((h * h).astype(BF16), w2_ref[c * fc:(c + 1) * fc, :],
                            preferred_element_type=F32)
    return _layer_norm(DN_ALPHA * x1 + acc, g2, b2)


def _attn_tail_kernel(oa_ref, ob_ref, x_ref, wo_ref, g1_ref, b1_ref, w1_ref, w2_ref, g2_ref, b2_ref,
                      out_ref):
    mix = (jnp.dot(oa_ref[...], wo_ref[:A_Q_DIM, :], preferred_element_type=F32)
           + jnp.dot(ob_ref[...], wo_ref[A_Q_DIM:, :], preferred_element_type=F32))
    out_ref[...] = _post_ln_mlp(x_ref[...], mix, g1_ref[...], b1_ref[...], w1_ref, w2_ref,
                                g2_ref[...], b2_ref[...])


def _attn_tail(oa, ob, x2d, wo, g1, b1, w1, w2, g2, b2):
    m = x2d.shape[0]
    tm = 512
    row = lambda i: (i, 0)
    vec = lambda a: a.reshape(1, -1).astype(F32)
    return pl.pallas_call(
        _attn_tail_kernel,
        grid=(m // tm,),
        in_specs=[pl.BlockSpec((tm, A_Q_DIM), row), pl.BlockSpec((tm, B_DIM), row),
                  pl.BlockSpec((tm, D_MODEL), row), _resident(wo.shape),
                  _resident((1, D_MODEL)), _resident((1, D_MODEL)),
                  _resident(w1.shape), _resident(w2.shape),
                  _resident((1, D_MODEL)), _resident((1, D_MODEL))],
        out_specs=pl.BlockSpec((tm, D_MODEL), row),
        out_shape=jax.ShapeDtypeStruct((m, D_MODEL), F32),
        compiler_params=_params("parallel"),
        name="attn_out_mlp",
    )(oa, ob, x2d, wo, vec(g1), vec(b1), w1, w2, vec(g2), vec(b2))


def _ssm_tail_kernel(yf_ref, yb_ref, xs_ref, z_ref, dskip_ref, nw_ref, x_ref, wo_ref, g1_ref, b1_ref,
                     w1_ref, w2_ref, g2_ref, b2_ref, out_ref):
    y = yf_ref[...].astype(F32) + yb_ref[...].astype(F32) + xs_ref[...].astype(F32) * dskip_ref[...]
    y = y * _silu(z_ref[...].astype(F32))
    parts = []
    for g in range(SSM_GROUPS):
        yg = y[:, g * SSM_GROUP_W:(g + 1) * SSM_GROUP_W]
        ms = jnp.mean(yg * yg, axis=-1, keepdims=True)
        parts.append(yg * lax.rsqrt(ms + RMS_EPS))
    yn = (jnp.concatenate(parts, axis=-1) * nw_ref[...]).astype(BF16)
    mix = jnp.dot(yn, wo_ref[...], preferred_element_type=F32)
    out_ref[...] = _post_ln_mlp(x_ref[...], mix, g1_ref[...], b1_ref[...], w1_ref, w2_ref,
                                g2_ref[...], b2_ref[...])


def _ssm_tail(yf, yb, xbc, z, d_skip, norm_w, x2d, wo, g1, b1, w1, w2, g2, b2):
    m = x2d.shape[0]
    tm = 512
    row = lambda i: (i, 0)
    vec = lambda a: a.reshape(1, -1).astype(F32)
    wide = pl.BlockSpec((tm, SSM_D_INNER), row)
    dskip_e = jnp.repeat(d_skip.astype(F32), SSM_D_INNER // SSM_HEADS).reshape(1, -1)
    return pl.pallas_call(
        _ssm_tail_kernel,
        grid=(m // tm,),
        in_specs=[wide, wide, wide, wide, _resident((1, SSM_D_INNER)), _resident((1, SSM_D_INNER)),
                  pl.BlockSpec((tm, D_MODEL), row), _resident(wo.shape),
                  _resident((1, D_MODEL)), _resident((1, D_MODEL)),
                  _resident(w1.shape), _resident(w2.shape),
                  _resident((1, D_MODEL)), _resident((1, D_MODEL))],
        out_specs=pl.BlockSpec((tm, D_MODEL), row),
        out_shape=jax.ShapeDtypeStruct((m, D_MODEL), F32),
        compiler_params=_params("parallel"),
        name="ssm_out_mlp",
    )(yf, yb, xbc, z, dskip_e, vec(norm_w), x2d, wo, vec(g1), vec(b1), w1, w2, vec(g2), vec(b2))


def _ssm_in_kernel(x_ref, w_ref, z_ref, xbc_ref, dt_ref):
    xb = x_ref[...].astype(BF16)
    nc = 1024
    for c in range(SSM_D_INNER // nc):
        z_ref[:, c * nc:(c + 1) * nc] = jnp.dot(
            xb, w_ref[:, c * nc:(c + 1) * nc], preferred_element_type=F32).astype(BF16)
    for c in range(SSM_CONV_DIM // nc):
        c0 = SSM_D_INNER + c * nc
        xbc_ref[:, c * nc:(c + 1) * nc] = jnp.dot(xb, w_ref[:, c0:c0 + nc], preferred_element_type=F32)
    c0 = SSM_D_INNER + SSM_CONV_DIM
    dt_ref[...] = jnp.dot(xb, w_ref[:, c0:c0 + LANES], preferred_element_type=F32)


def _ssm_in_proj(x2d, w_pad):
    m = x2d.shape[0]
    tm = 512
    row = lambda i: (i, 0)
    return pl.pallas_call(
        _ssm_in_kernel,
        grid=(m // tm,),
        in_specs=[pl.BlockSpec((tm, D_MODEL), row), _resident(w_pad.shape)],
        out_specs=[pl.BlockSpec((tm, SSM_D_INNER), row), pl.BlockSpec((tm, SSM_CONV_DIM), row),
                   pl.BlockSpec((tm, LANES), row)],
        out_shape=[jax.ShapeDtypeStruct((m, SSM_D_INNER), BF16),
                   jax.ShapeDtypeStruct((m, SSM_CONV_DIM), F32),
                   jax.ShapeDtypeStruct((m, LANES), F32)],
        compiler_params=_params("parallel"),
        name="ssm_in_proj",
    )(x2d, w_pad)


def _conv_kernel(tiles_per_seq, prev_ref, main_ref, next_ref, w_ref, b_ref, o_ref):
    i = pl.program_id(0)
    tm = main_ref.shape[0]
    pos = i % tiles_per_seq
    prev = jnp.where(pos > 0, prev_ref[...], 0.0)
    nxt = jnp.where(pos < tiles_per_seq - 1, next_ref[...], 0.0)
    ext = jnp.concatenate([prev, main_ref[...], nxt], axis=0)
    n_ext = tm + 2 * HALO
    pad = SSM_CONV // 2
    acc = jnp.zeros(main_ref.shape, F32) + b_ref[...]
    for k in range(SSM_CONV):
        shift = (pad - k) % n_ext
        tap = ext if shift == 0 else pltpu.roll(ext, shift, 0)
        acc = acc + tap[HALO:HALO + tm] * w_ref[k:k + 1, :]
    o_ref[...] = _silu(acc).astype(BF16)


def _conv_silu(xbc_raw, conv_w, conv_b, seq):
    m, n = xbc_raw.shape
    tm, nc = 512, 1024
    tpb = seq // tm
    hb = tm // HALO
    last = m // HALO - 1
    return pl.pallas_call(
        functools.partial(_conv_kernel, tpb),
        grid=(m // tm, n // nc),
        in_specs=[pl.BlockSpec((HALO, nc), lambda i, j: (jnp.maximum(i * hb - 1, 0), j)),
                  pl.BlockSpec((tm, nc), lambda i, j: (i, j)),
                  pl.BlockSpec((HALO, nc), lambda i, j: (jnp.minimum((i + 1) * hb, last), j)),
                  pl.BlockSpec((SSM_CONV, nc), lambda i, j: (0, j)),
                  pl.BlockSpec((1, nc), lambda i, j: (0, j))],
        out_specs=pl.BlockSpec((tm, nc), lambda i, j: (i, j)),
        out_shape=jax.ShapeDtypeStruct((m, n), BF16),
        compiler_params=_params("parallel", "parallel"),
        name="conv_silu",
    )(xbc_raw, xbc_raw, xbc_raw, conv_w.astype(F32), conv_b.reshape(1, -1).astype(F32))


def _ssd_kernel(reverse, xs_ref, b_ref, c_ref, dtraw_ref, dtb_ref, alog_ref, y_ref, state_ref, act_ref):
    L = SSM_CHUNK
    hpg = SSM_HEADS_PER_GROUP
    off = SSM_HEADS if reverse else 0

    @pl.when(pl.program_id(1) == 0)
    def _():
        state_ref[...] = jnp.zeros_like(state_ref)

    v = dtraw_ref[...] + dtb_ref[...]
    dt = jnp.maximum(v, 0.0) + jnp.log1p(jnp.exp(-jnp.abs(v)))
    a = dt * (-jnp.exp(alog_ref[...]))
    li = lax.broadcasted_iota(jnp.int32, (L, L), 0)
    si = lax.broadcasted_iota(jnp.int32, (L, L), 1)
    ac = a
    s = 1
    while s < L:
        if reverse:
            ac = ac + jnp.where(li < L - s, pltpu.roll(ac, L - s, 0), 0.0)
        else:
            ac = ac + jnp.where(li >= s, pltpu.roll(ac, s, 0), 0.0)
        s *= 2
    act_ref[...] = ac.T
    tot = ac[0:1, :] if reverse else ac[L - 1:L, :]
    tri = (si >= li) if reverse else (li >= si)
    head_of_lane = lax.broadcasted_iota(jnp.int32, (L, SSM_GROUP_W), 1) // (SSM_GROUP_W // hpg)

    def expand(col_src, g):
        w = SSM_GROUP_W // hpg
        return jnp.concatenate(
            [jnp.broadcast_to(col_src[:, off + g * hpg + r: off + g * hpg + r + 1], (col_src.shape[0], w))
             for r in range(hpg)], axis=1)

    for g in range(SSM_GROUPS):
        bg = b_ref[:, g * SSM_STATE:(g + 1) * SSM_STATE]
        cg = c_ref[:, g * SSM_STATE:(g + 1) * SSM_STATE]
        xg = xs_ref[:, g * SSM_GROUP_W:(g + 1) * SSM_GROUP_W].astype(F32)
        cb = lax.dot_general(cg, bg, (((1,), (1,)), ((), ())), preferred_element_type=F32)
        ac_e = expand(ac, g)
        tot_e = expand(tot, g)
        xc = xg * expand(dt, g)
        xcb = xc.astype(BF16)
        prev = state_ref[g]
        y_off = jnp.dot(cg, prev.astype(BF16), preferred_element_type=F32) * jnp.exp(ac_e)
        ms = []
        for r in range(hpg):
            h = off + g * hpg + r
            col = jnp.broadcast_to(ac[:, h:h + 1], (L, L))
            rowv = act_ref[h:h + 1, :]
            lm = jnp.exp(jnp.where(tri, col - rowv, NEG_INF))
            ms.append((cb * lm).astype(BF16))
        lhs = jnp.concatenate(ms, axis=1)
        rhs = jnp.concatenate([jnp.where(head_of_lane == r, xcb, jnp.zeros_like(xcb)) for r in range(hpg)],
                              axis=0)
        y_diag = jnp.dot(lhs, rhs, preferred_element_type=F32)
        y_ref[:, g * SSM_GROUP_W:(g + 1) * SSM_GROUP_W] = (y_diag + y_off).astype(y_ref.dtype)
        xd = (xc * jnp.exp(tot_e - ac_e)).astype(BF16)
        contrib = lax.dot_general(bg, xd, (((0,), (0,)), ((), ())), preferred_element_type=F32)
        state_ref[g] = prev * jnp.exp(tot_e) + contrib


def _ssd_scan(xbc, dt_raw, dt_bias_row, alog_row, bsz, seq, reverse):
    m = xbc.shape[0]
    nc = seq // SSM_CHUNK
    L = SSM_CHUNK
    n_bc = SSM_GROUPS * SSM_STATE

    def chunk(b, c):
        return b * nc + (nc - 1 - c if reverse else c)

    return pl.pallas_call(
        functools.partial(_ssd_kernel, reverse),
        grid=(bsz, nc),
        in_specs=[pl.BlockSpec((L, SSM_D_INNER), lambda b, c: (chunk(b, c), 0)),
                  pl.BlockSpec((L, n_bc), lambda b, c: (chunk(b, c), SSM_D_INNER // n_bc)),
                  pl.BlockSpec((L, n_bc), lambda b, c: (chunk(b, c), SSM_D_INNER // n_bc + 1)),
                  pl.BlockSpec((L, LANES), lambda b, c: (chunk(b, c), 0)),
                  _resident((1, LANES)), _resident((1, LANES))],
        out_specs=pl.BlockSpec((L, SSM_D_INNER), lambda b, c: (chunk(b, c), 0)),
        out_shape=jax.ShapeDtypeStruct((m, SSM_D_INNER), BF16),
        scratch_shapes=[pltpu.VMEM((SSM_GROUPS, SSM_STATE, SSM_GROUP_W), F32),
                        pltpu.VMEM((LANES, L), F32)],
        compiler_params=_params("parallel", "arbitrary"),
        name="ssd_bwd" if reverse else "ssd_fwd",
    )(xbc, xbc, xbc, dt_raw, dt_bias_row, alog_row)


def kernel(x, attn_w_in, attn_sink, attn_rpb, attn_w_out, ssm_w_in, ssm_conv_w, ssm_conv_b, ssm_dt_bias,
           ssm_A_log, ssm_D, ssm_norm_w, ssm_w_out, mlp_w1, mlp_w2, ln1_g, ln1_b, ln2_g, ln2_b):
    bsz, seq, _ = x.shape
    x2d = x.reshape(bsz * seq, D_MODEL)

    qa, ka, va, qb, kb, vb = _attn_in_proj(x2d, attn_w_in[0].astype(BF16), seq)
    oa = _win_attn(qa, ka, va, attn_sink[0], bsz, seq)
    ob = _na_attn(qb, kb, vb, attn_rpb[0], bsz, seq)
    x2d = _attn_tail(oa, ob, x2d, attn_w_out[0].astype(BF16), ln1_g[0], ln1_b[0],
                     mlp_w1[0].astype(BF16), mlp_w2[0].astype(BF16), ln2_g[0], ln2_b[0])

    n_dt = 2 * SSM_HEADS
    w_pad = jnp.pad(ssm_w_in[0].astype(BF16), ((0, 0), (0, LANES - n_dt)))
    z, xbc_raw, dt_raw = _ssm_in_proj(x2d, w_pad)
    xbc = _conv_silu(xbc_raw, ssm_conv_w[0], ssm_conv_b[0], seq)
    lane_pad = lambda a: jnp.pad(a.reshape(1, n_dt).astype(F32), ((0, 0), (0, LANES - n_dt)))
    dt_bias_row = lane_pad(ssm_dt_bias[0])
    alog_row = lane_pad(ssm_A_log[0])
    y_b = _ssd_scan(xbc, dt_raw, dt_bias_row, alog_row, bsz, seq, reverse=True)
    y_f = _ssd_scan(xbc, dt_raw, dt_bias_row, alog_row, bsz, seq, reverse=False)
    x2d = _ssm_tail(y_f, y_b, xbc, z, ssm_D[0], ssm_norm_w[0], x2d, ssm_w_out[0].astype(BF16),
                    ln1_g[1], ln1_b[1], mlp_w1[1].astype(BF16), mlp_w2[1].astype(BF16),
                    ln2_g[1], ln2_b[1])
    return x2d.reshape(bsz, seq, D_MODEL)
```

```python
import functools

import numpy as np
import jax
import jax.numpy as jnp
from jax import lax
from jax.experimental import pallas as pl
from jax.experimental.pallas import tpu as pltpu

F32 = jnp.float32
BF16 = jnp.bfloat16

D_MODEL = 1024
HEAD_DIM = 64
A_Q_HEADS = 8
A_KV_HEADS = 2
A_BLOCK = 128
WIN_QBLOCKS = 4
ROPE_THETA = 10000.0
B_HEADS = 8
GRID_W = 64
NA_KH = 8
NA_KW = 16
NA_QROWS = 4
NA_KROWS = 12
A_Q_DIM = A_Q_HEADS * HEAD_DIM
A_KV_DIM = A_KV_HEADS * HEAD_DIM
B_DIM = B_HEADS * HEAD_DIM
SSM_D_INNER = 2 * D_MODEL
SSM_HEADS = 32
SSM_GROUPS = 8
SSM_STATE = 128
SSM_CONV = 5
SSM_CHUNK = 128
SSM_GROUP_W = SSM_D_INNER // SSM_GROUPS
SSM_HEADS_PER_GROUP = SSM_HEADS // SSM_GROUPS
SSM_CONV_DIM = SSM_D_INNER + 2 * SSM_GROUPS * SSM_STATE
D_FF = 4 * D_MODEL
DEPTH = 2
DN_ALPHA = (2 * DEPTH) ** 0.25
LN_EPS = 1e-5
RMS_EPS = 1e-5
NEG_INF = -1e30
LOG2E = 1.4426950408889634
LANES = 128
HALO = 8

VMEM_LIMIT = 56 * 1024 * 1024


def _params(*sem):
    return pltpu.CompilerParams(dimension_semantics=sem, vmem_limit_bytes=VMEM_LIMIT)


def _resident(shape):
    nd = len(shape)
    return pl.BlockSpec(shape, lambda *_: (0,) * nd, pipeline_mode=pl.Buffered(1))


def _layer_norm(v, g, b):
    mu = jnp.mean(v, axis=-1, keepdims=True)
    d = v - mu
    var = jnp.mean(d * d, axis=-1, keepdims=True)
    return d * lax.rsqrt(var + LN_EPS) * g + b


def _silu(v):
    return v / (1.0 + jnp.exp2(v * (-LOG2E)))


def _attn_in_kernel(x_ref, w_ref, cos_ref, sin_ref, qa_ref, ka_ref, va_ref, qb_ref, kb_ref, vb_ref):
    tm = x_ref.shape[0]
    xb = x_ref[...].astype(BF16)
    cos = cos_ref[...]
    sin = sin_ref[...]
    lane = lax.broadcasted_iota(jnp.int32, (tm, LANES), 1)
    first_half = (lane % HEAD_DIM) < (HEAD_DIM // 2)
    scale = HEAD_DIM ** -0.5

    def rope(h):
        partner = jnp.where(first_half, pltpu.roll(h, LANES - HEAD_DIM // 2, 1),
                            pltpu.roll(h, HEAD_DIM // 2, 1))
        return h * cos + partner * sin

    n_rope = A_Q_DIM + A_KV_DIM
    h = jnp.dot(xb, w_ref[:, :n_rope], preferred_element_type=F32)
    for j in range(A_Q_DIM // LANES):
        qa_ref[:, j * LANES:(j + 1) * LANES] = (rope(h[:, j * LANES:(j + 1) * LANES]) * scale).astype(BF16)
    for j in range(A_KV_DIM // LANES):
        c0 = A_Q_DIM + j * LANES
        ka_ref[j * LANES:(j + 1) * LANES, :] = rope(h[:, c0:c0 + LANES]).T.astype(BF16)
    c0 = n_rope
    va_ref[...] = jnp.dot(xb, w_ref[:, c0:c0 + A_KV_DIM], preferred_element_type=F32).astype(BF16)
    c0 += A_KV_DIM
    qb_ref[...] = (jnp.dot(xb, w_ref[:, c0:c0 + B_DIM], preferred_element_type=F32) * scale).astype(BF16)
    c0 += B_DIM
    kb_ref[...] = jnp.dot(xb, w_ref[:, c0:c0 + B_DIM], preferred_element_type=F32).astype(BF16).T
    c0 += B_DIM
    vb_ref[...] = jnp.dot(xb, w_ref[:, c0:c0 + B_DIM], preferred_element_type=F32).astype(BF16)


def _attn_in_proj(x2d, w_bf16, seq):
    m = x2d.shape[0]
    tm = 512
    half = HEAD_DIM // 2
    inv = ROPE_THETA ** (-jnp.arange(half, dtype=F32) / half)
    ang = jnp.arange(seq, dtype=F32)[:, None] * inv[None, :]
    reps = LANES // half
    cos_t = jnp.tile(jnp.cos(ang), (1, reps))
    sign = jnp.tile(jnp.concatenate([-jnp.ones((half,), F32), jnp.ones((half,), F32)]), LANES // HEAD_DIM)
    sin_t = jnp.tile(jnp.sin(ang), (1, reps)) * sign[None, :]
    tpb = seq // tm
    row = lambda i: (i, 0)
    pos = lambda i: (i % tpb, 0)
    col = lambda i: (0, i)
    tok = lambda n: (pl.BlockSpec((tm, n), row), jax.ShapeDtypeStruct((m, n), BF16))
    chan = lambda n: (pl.BlockSpec((n, tm), col), jax.ShapeDtypeStruct((n, m), BF16))
    outs = [tok(A_Q_DIM), chan(A_KV_DIM), tok(A_KV_DIM), tok(B_DIM), chan(B_DIM), tok(B_DIM)]
    return pl.pallas_call(
        _attn_in_kernel,
        grid=(m // tm,),
        in_specs=[pl.BlockSpec((tm, D_MODEL), row), _resident(w_bf16.shape),
                  pl.BlockSpec((tm, LANES), pos), pl.BlockSpec((tm, LANES), pos)],
        out_specs=[o[0] for o in outs],
        out_shape=[o[1] for o in outs],
        compiler_params=_params("parallel"),
        name="attn_in_proj",
    )(x2d, w_bf16, cos_t, sin_t)


def _win_attn_kernel(sink_ref, q_ref, kp_ref, kc_ref, kn_ref, vp_ref, vc_ref, vn_ref, o_ref):
    i = pl.program_id(1)
    ni = pl.num_programs(1)
    blk = A_BLOCK
    rep = A_Q_HEADS // A_KV_HEADS
    qi = lax.broadcasted_iota(jnp.int32, (rep * blk, 3 * blk), 0) % blk
    ki = lax.broadcasted_iota(jnp.int32, (rep * blk, 3 * blk), 1)
    band = jnp.abs(qi + blk - ki) <= blk
    kt_all = jnp.concatenate([kp_ref[...], kc_ref[...], kn_ref[...]], axis=1)
    v_all = jnp.concatenate([vp_ref[...], vc_ref[...], vn_ref[...]], axis=0)
    for sb in range(WIN_QBLOCKS):
        valid = band
        if sb == 0:
            valid = valid & ((ki >= blk) | (i > 0))
        if sb == WIN_QBLOCKS - 1:
            valid = valid & ((ki < 2 * blk) | (i < ni - 1))
        rows = slice(sb * blk, (sb + 1) * blk)
        for g in range(A_KV_HEADS):
            kg_t = kt_all[g * HEAD_DIM:(g + 1) * HEAD_DIM, sb * blk:(sb + 3) * blk]
            vg = v_all[sb * blk:(sb + 3) * blk, g * HEAD_DIM:(g + 1) * HEAD_DIM]
            heads = [g * rep + r for r in range(rep)]
            q4 = jnp.concatenate([q_ref[rows, h * HEAD_DIM:(h + 1) * HEAD_DIM] for h in heads], axis=0)
            s = jnp.dot(q4, kg_t, preferred_element_type=F32)
            s = jnp.where(valid, s, NEG_INF)
            for r, h in enumerate(heads):
                sr = s[r * blk:(r + 1) * blk]
                sink = sink_ref[h]
                mx = jnp.maximum(jnp.max(sr, axis=-1, keepdims=True), sink)
                p = jnp.exp(sr - mx)
                den = jnp.sum(p, axis=-1, keepdims=True) + jnp.exp(sink - mx)
                o = jnp.dot(p.astype(BF16), vg, preferred_element_type=F32) / den
                o_ref[rows, h * HEAD_DIM:(h + 1) * HEAD_DIM] = o.astype(BF16)


def _win_attn(qa, ka, va, sink, bsz, seq):
    nb = seq // A_BLOCK
    tq = WIN_QBLOCKS * A_BLOCK
    ni = seq // tq
    cur = lambda b, i: (b * ni + i, 0)
    prv = lambda b, i: (b * nb + jnp.maximum(i * WIN_QBLOCKS - 1, 0), 0)
    nxt = lambda b, i: (b * nb + jnp.minimum((i + 1) * WIN_QBLOCKS, nb - 1), 0)
    edge = lambda im: pl.BlockSpec((A_BLOCK, A_KV_DIM), im)
    body = pl.BlockSpec((tq, A_KV_DIM), cur)
    flip = lambda im: (lambda b, i: im(b, i)[::-1])
    edge_t = lambda im: pl.BlockSpec((A_KV_DIM, A_BLOCK), flip(im))
    body_t = pl.BlockSpec((A_KV_DIM, tq), flip(cur))
    return pl.pallas_call(
        _win_attn_kernel,
        grid=(bsz, ni),
        in_specs=[pl.BlockSpec(memory_space=pltpu.SMEM),
                  pl.BlockSpec((tq, A_Q_DIM), cur),
                  edge_t(prv), body_t, edge_t(nxt), edge(prv), body, edge(nxt)],
        out_specs=pl.BlockSpec((tq, A_Q_DIM), cur),
        out_shape=jax.ShapeDtypeStruct(qa.shape, BF16),
        compiler_params=_params("parallel", "parallel"),
        name="window_attn",
    )(sink.astype(F32), qa, ka, ka, ka, va, va, va)


def _na_row_tables(rows):
    nj = rows // NA_QROWS
    assert rows >= NA_KROWS + NA_QROWS and rows % NA_QROWS == 0

    def tile(j):
        ws = min(max(NA_QROWS * j - NA_KH // 2, 0), rows - NA_KROWS)
        r = NA_QROWS * j + np.arange(NA_QROWS)[:, None]
        rk = ws + np.arange(NA_KROWS)[None, :]
        rs = np.clip(r - NA_KH // 2, 0, rows - NA_KH)
        return (rk >= rs) & (rk < rs + NA_KH), rk - r + NA_KH - 1

    cases = [tile(0), tile(1), tile(nj - 1)]
    for j in range(1, nj - 1):
        ok, dr = tile(j)
        assert np.array_equal(ok, cases[1][0]) and np.array_equal(np.where(ok, dr, 0), np.where(ok, cases[1][1], 0))
    return np.stack([c[0] for c in cases]), np.stack([c[1] for c in cases])


def _na_bias(rpb, rows):
    ok_r, dr = _na_row_tables(rows)
    w = GRID_W
    pad = w - NA_KW
    padded = jnp.pad(rpb.astype(F32), ((0, 0), (0, 0), (pad, pad)))
    toep = jnp.stack([padded[:, :, w - 1 - cq:2 * w - 1 - cq] for cq in range(w)], axis=2)
    cq = np.arange(w)[:, None]
    ck = np.arange(w)[None, :]
    cs = np.clip(cq - NA_KW // 2, 0, w - NA_KW)
    toep = jnp.where((ck >= cs) & (ck < cs + NA_KW), toep, NEG_INF)
    outside = jnp.full((B_HEADS, w, w), NEG_INF, F32)
    tiles = []
    for case in range(ok_r.shape[0]):
        q_rows = []
        for rq in range(NA_QROWS):
            blocks = [toep[:, int(dr[case, rq, rk])] if ok_r[case, rq, rk] else outside
                      for rk in range(NA_KROWS)]
            q_rows.append(jnp.concatenate(blocks, axis=-1))
        tiles.append(jnp.concatenate(q_rows, axis=1))
    return jnp.stack(tiles)


def _na_kernel(q_ref, k0_ref, k1_ref, k2_ref, v0_ref, v1_ref, v2_ref, bias_ref, o_ref):
    kt_all = jnp.concatenate([k0_ref[...], k1_ref[...], k2_ref[...]], axis=1)
    v_all = jnp.concatenate([v0_ref[...], v1_ref[...], v2_ref[...]], axis=0)
    for h in range(B_HEADS):
        sl = slice(h * HEAD_DIM, (h + 1) * HEAD_DIM)
        s = jnp.dot(q_ref[:, sl], kt_all[sl, :], preferred_element_type=F32)
        s = s + bias_ref[0, h]
        mx = jnp.max(s, axis=-1, keepdims=True)
        p = jnp.exp(s - mx)
        den = jnp.sum(p, axis=-1, keepdims=True)
        o = jnp.dot(p.astype(BF16), v_all[:, sl], preferred_element_type=F32) / den
        o_ref[:, sl] = o.astype(BF16)


def _na_attn(qb, kb, vb, rpb, bsz, seq):
    rows = seq // GRID_W
    nj = rows // NA_QROWS
    bias = _na_bias(rpb, rows)
    tq = NA_QROWS * GRID_W
    kblocks = NA_KROWS // NA_QROWS
    per_b = seq // tq
    qmap = lambda b, j: (b * per_b + j, 0)

    def kmap(i):
        return lambda b, j: (b * per_b + jnp.clip(j - 1, 0, nj - kblocks) + i, 0)

    case = lambda b, j: (jnp.where(j == 0, 0, jnp.where(j == nj - 1, 2, 1)), 0, 0, 0)
    kv = [pl.BlockSpec((tq, B_DIM), kmap(i)) for i in range(kblocks)]
    kv_t = [pl.BlockSpec((B_DIM, tq), (lambda f: (lambda b, j: f(b, j)[::-1]))(kmap(i))) for i in range(kblocks)]
    return pl.pallas_call(
        _na_kernel,
        grid=(bsz, nj),
        in_specs=[pl.BlockSpec((tq, B_DIM), qmap)] + kv_t + kv
                 + [pl.BlockSpec((1, B_HEADS, tq, NA_KROWS * GRID_W), case)],
        out_specs=pl.BlockSpec((tq, B_DIM), qmap),
        out_shape=jax.ShapeDtypeStruct(qb.shape, BF16),
        compiler_params=_params("parallel", "arbitrary"),
        name="neighborhood_attn",
    )(qb, kb, kb, kb, vb, vb, vb, bias)


def _post_ln_mlp(x, mix, g1, b1, w1_ref, w2_ref, g2, b2):
    x1 = _layer_norm(DN_ALPHA * x + mix, g1, b1)
    xb = x1.astype(BF16)
    acc = jnp.zeros_like(x1)
    fc = 1024
    for c in range(D_FF // fc):
        h = jnp.dot(xb, w1_ref[:, c * fc:(c + 1) * fc], preferred_element_type=F32)
        h = jnp.maximum(h, 0.0)
        acc = acc + jnp.dot((h * h).astype(BF16), w2_ref[c * fc:(c + 1) * fc, :],
                            preferred_element_type=F32)
    return _layer_norm(DN_ALPHA * x1 + acc, g2, b2)


def _attn_tail_kernel(oa_ref, ob_ref, x_ref, wo_ref, g1_ref, b1_ref, w1_ref, w2_ref, g2_ref, b2_ref,
                      out_ref):
    mix = (jnp.dot(oa_ref[...], wo_ref[:A_Q_DIM, :], preferred_element_type=F32)
           + jnp.dot(ob_ref[...], wo_ref[A_Q_DIM:, :], preferred_element_type=F32))
    out_ref[...] = _post_ln_mlp(x_ref[...], mix, g1_ref[...], b1_ref[...], w1_ref, w2_ref,
                                g2_ref[...], b2_ref[...])


def _attn_tail(oa, ob, x2d, wo, g1, b1, w1, w2, g2, b2):
    m = x2d.shape[0]
    tm = 512
    row = lambda i: (i, 0)
    vec = lambda a: a.reshape(1, -1).astype(F32)
    return pl.pallas_call(
        _attn_tail_kernel,
        grid=(m // tm,),
        in_specs=[pl.BlockSpec((tm, A_Q_DIM), row), pl.BlockSpec((tm, B_DIM), row),
                  pl.BlockSpec((tm, D_MODEL), row), _resident(wo.shape),
                  _resident((1, D_MODEL)), _resident((1, D_MODEL)),
                  _resident(w1.shape), _resident(w2.shape),
                  _resident((1, D_MODEL)), _resident((1, D_MODEL))],
        out_specs=pl.BlockSpec((tm, D_MODEL), row),
        out_shape=jax.ShapeDtypeStruct((m, D_MODEL), F32),
        compiler_params=_params("parallel"),
        name="attn_out_mlp",
    )(oa, ob, x2d, wo, vec(g1), vec(b1), w1, w2, vec(g2), vec(b2))


def _ssm_tail_kernel(yf_ref, yb_ref, xs_ref, z_ref, dskip_ref, nw_ref, x_ref, wo_ref, g1_ref, b1_ref,
                     w1_ref, w2_ref, g2_ref, b2_ref, out_ref):
    y = yf_ref[...].astype(F32) + yb_ref[...].astype(F32) + xs_ref[...].astype(F32) * dskip_ref[...]
    y = y * _silu(z_ref[...].astype(F32))
    parts = []
    for g in range(SSM_GROUPS):
        yg = y[:, g * SSM_GROUP_W:(g + 1) * SSM_GROUP_W]
        ms = jnp.mean(yg * yg, axis=-1, keepdims=True)
        parts.append(yg * lax.rsqrt(ms + RMS_EPS))
    yn = (jnp.concatenate(parts, axis=-1) * nw_ref[...]).astype(BF16)
    mix = jnp.dot(yn, wo_ref[...], preferred_element_type=F32)
    out_ref[...] = _post_ln_mlp(x_ref[...], mix, g1_ref[...], b1_ref[...], w1_ref, w2_ref,
                                g2_ref[...], b2_ref[...])


def _ssm_tail(yf, yb, xbc, z, d_skip, norm_w, x2d, wo, g1, b1, w1, w2, g2, b2):
    m = x2d.shape[0]
    tm = 512
    row = lambda i: (i, 0)
    vec = lambda a: a.reshape(1, -1).astype(F32)
    wide = pl.BlockSpec((tm, SSM_D_INNER), row)
    dskip_e = jnp.repeat(d_skip.astype(F32), SSM_D_INNER // SSM_HEADS).reshape(1, -1)
    return pl.pallas_call(
        _ssm_tail_kernel,
        grid=(m // tm,),
        in_specs=[wide, wide, wide, wide, _resident((1, SSM_D_INNER)), _resident((1, SSM_D_INNER)),
                  pl.BlockSpec((tm, D_MODEL), row), _resident(wo.shape),
                  _resident((1, D_MODEL)), _resident((1, D_MODEL)),
                  _resident(w1.shape), _resident(w2.shape),
                  _resident((1, D_MODEL)), _resident((1, D_MODEL))],
        out_specs=pl.BlockSpec((tm, D_MODEL), row),
        out_shape=jax.ShapeDtypeStruct((m, D_MODEL), F32),
        compiler_params=_params("parallel"),
        name="ssm_out_mlp",
    )(yf, yb, xbc, z, dskip_e, vec(norm_w), x2d, wo, vec(g1), vec(b1), w1, w2, vec(g2), vec(b2))


def _ssm_in_kernel(tiles_per_seq, prev_ref, x_ref, next_ref, w_ref, cw_ref, cb_ref, z_ref, xbc_ref, dt_ref):
    tm = x_ref.shape[0]
    pos = pl.program_id(0) % tiles_per_seq
    x = x_ref[...]
    xb = x.astype(BF16)
    nc = 1024
    for c in range(SSM_D_INNER // nc):
        z_ref[:, c * nc:(c + 1) * nc] = jnp.dot(
            xb, w_ref[:, c * nc:(c + 1) * nc], preferred_element_type=F32).astype(BF16)
    c0 = SSM_D_INNER + SSM_CONV_DIM
    dt_ref[...] = jnp.dot(xb, w_ref[:, c0:c0 + LANES], preferred_element_type=F32)
    prev = jnp.where(pos > 0, prev_ref[...], 0.0)
    nxt = jnp.where(pos < tiles_per_seq - 1, next_ref[...], 0.0)
    xe = jnp.concatenate([prev, x, nxt], axis=0).astype(BF16)
    n_ext = tm + 2 * HALO
    pad = SSM_CONV // 2
    for c in range(SSM_CONV_DIM // nc):
        cols = slice(c * nc, (c + 1) * nc)
        raw = jnp.dot(xe, w_ref[:, SSM_D_INNER + c * nc:SSM_D_INNER + (c + 1) * nc],
                      preferred_element_type=F32)
        acc = raw[HALO:HALO + tm] * cw_ref[pad:pad + 1, cols] + cb_ref[:, cols]
        for k in range(SSM_CONV):
            if k != pad:
                tap = pltpu.roll(raw, (pad - k) % n_ext, 0)
                acc = acc + tap[HALO:HALO + tm] * cw_ref[k:k + 1, cols]
        xbc_ref[:, cols] = _silu(acc).astype(BF16)


def _ssm_in_proj(x2d, w_pad, conv_w, conv_b, seq):
    m = x2d.shape[0]
    tm = 512
    tpb = seq // tm
    hb = tm // HALO
    last = m // HALO - 1
    row = lambda i: (i, 0)
    return pl.pallas_call(
        functools.partial(_ssm_in_kernel, tpb),
        grid=(m // tm,),
        in_specs=[pl.BlockSpec((HALO, D_MODEL), lambda i: (jnp.maximum(i * hb - 1, 0), 0)),
                  pl.BlockSpec((tm, D_MODEL), row),
                  pl.BlockSpec((HALO, D_MODEL), lambda i: (jnp.minimum((i + 1) * hb, last), 0)),
                  _resident(w_pad.shape), _resident((SSM_CONV, SSM_CONV_DIM)), _resident((1, SSM_CONV_DIM))],
        out_specs=[pl.BlockSpec((tm, SSM_D_INNER), row), pl.BlockSpec((tm, SSM_CONV_DIM), row),
                   pl.BlockSpec((tm, LANES), row)],
        out_shape=[jax.ShapeDtypeStruct((m, SSM_D_INNER), BF16),
                   jax.ShapeDtypeStruct((m, SSM_CONV_DIM), BF16),
                   jax.ShapeDtypeStruct((m, LANES), F32)],
        compiler_params=_params("parallel"),
        name="ssm_in_conv",
    )(x2d, x2d, x2d, w_pad, conv_w.astype(F32), conv_b.reshape(1, -1).astype(F32))


def _ssd_chunk(reverse, first, xs_ref, b_ref, c_ref, dtraw_ref, dtb_ref, alog_ref, y_ref,
               state_ref, blk_ref, rt_ref):
    L = SSM_CHUNK
    hpg = SSM_HEADS_PER_GROUP
    hw = SSM_GROUP_W // hpg
    off = SSM_HEADS if reverse else 0

    @pl.when(first)
    def _():
        state_ref[...] = jnp.zeros_like(state_ref)
        blk_ref[...] = jnp.zeros_like(blk_ref)

    v = dtraw_ref[...] + dtb_ref[...]
    dt = jnp.maximum(v, 0.0) + jnp.log1p(jnp.exp(-jnp.abs(v)))
    a2 = dt * (-jnp.exp(alog_ref[...])) * LOG2E
    li = lax.broadcasted_iota(jnp.int32, (L, L), 0)
    si = lax.broadcasted_iota(jnp.int32, (L, L), 1)
    ac2 = a2
    s = 1
    while s < L:
        if reverse:
            ac2 = ac2 + jnp.where(li < L - s, pltpu.roll(ac2, L - s, 0), 0.0)
        else:
            ac2 = ac2 + jnp.where(li >= s, pltpu.roll(ac2, s, 0), 0.0)
        s *= 2
    tot2 = ac2[0:1, :] if reverse else ac2[L - 1:L, :]
    ldt = jnp.log2(dt)
    rt_ref[...] = (ac2 - ldt).T
    from_start = jnp.exp2(ac2)
    to_end = jnp.exp2(tot2 - ac2 + ldt)
    chunk_decay = jnp.exp2(tot2)
    tri = (si >= li) if reverse else (li >= si)

    lane = lax.broadcasted_iota(jnp.int32, (L, LANES), 1)

    def expand(src, g):
        if src.shape[0] != L:
            return jnp.concatenate(
                [jnp.broadcast_to(src[:, off + g * hpg + r:off + g * hpg + r + 1], (src.shape[0], hw))
                 for r in range(hpg)], axis=1)
        return jnp.concatenate(
            [jnp.take_along_axis(src, off + g * hpg + 2 * j + lane // hw, axis=1) for j in range(hpg // 2)],
            axis=1)

    for g in range(SSM_GROUPS):
        bg = b_ref[:, g * SSM_STATE:(g + 1) * SSM_STATE]
        cg = c_ref[:, g * SSM_STATE:(g + 1) * SSM_STATE]
        cb = lax.dot_general(cg, bg, (((1,), (1,)), ((), ())), preferred_element_type=F32)
        ms = []
        for r in range(hpg):
            h = off + g * hpg + r
            col = jnp.broadcast_to(ac2[:, h:h + 1], (L, L))
            lm = jnp.exp2(jnp.where(tri, col - rt_ref[h:h + 1, :], NEG_INF))
            ms.append((cb * lm).astype(BF16))
            c0 = g * SSM_GROUP_W + r * hw
            blk_ref[g, r * L:(r + 1) * L, r * hw:(r + 1) * hw] = xs_ref[:, c0:c0 + hw]
        prev = state_ref[g]
        y = jnp.dot(jnp.concatenate(ms, axis=1), blk_ref[g], preferred_element_type=F32)
        y = y + jnp.dot(cg, prev.astype(BF16), preferred_element_type=F32) * expand(from_start, g)
        y_ref[:, g * SSM_GROUP_W:(g + 1) * SSM_GROUP_W] = y.astype(y_ref.dtype)
        xg = xs_ref[:, g * SSM_GROUP_W:(g + 1) * SSM_GROUP_W].astype(F32)
        xd = (xg * expand(to_end, g)).astype(BF16)
        contrib = lax.dot_general(bg, xd, (((0,), (0,)), ((), ())), preferred_element_type=F32)
        state_ref[g] = prev * expand(chunk_decay, g) + contrib


def _ssd_kernel(xf_ref, bf_ref, cf_ref, dtf_ref, xb_ref, bb_ref, cb_ref, dtb_ref, bias_ref, alog_ref,
                yf_ref, yb_ref, sf_ref, kf_ref, rf_ref, sb_ref, kb_ref, rb_ref):
    first = pl.program_id(1) == 0
    _ssd_chunk(False, first, xf_ref, bf_ref, cf_ref, dtf_ref, bias_ref, alog_ref, yf_ref,
               sf_ref, kf_ref, rf_ref)
    _ssd_chunk(True, first, xb_ref, bb_ref, cb_ref, dtb_ref, bias_ref, alog_ref, yb_ref,
               sb_ref, kb_ref, rb_ref)


def _ssd_scan(xbc, dt_raw, dt_bias_row, alog_row, bsz, seq):
    m = xbc.shape[0]
    nc = seq // SSM_CHUNK
    L = SSM_CHUNK
    n_bc = SSM_GROUPS * SSM_STATE
    fwd = lambda b, c: b * nc + c
    bwd = lambda b, c: b * nc + nc - 1 - c

    def operands(ch):
        return [pl.BlockSpec((L, SSM_D_INNER), lambda b, c: (ch(b, c), 0)),
                pl.BlockSpec((L, n_bc), lambda b, c: (ch(b, c), SSM_D_INNER // n_bc)),
                pl.BlockSpec((L, n_bc), lambda b, c: (ch(b, c), SSM_D_INNER // n_bc + 1)),
                pl.BlockSpec((L, LANES), lambda b, c: (ch(b, c), 0))]

    scratch = [pltpu.VMEM((SSM_GROUPS, SSM_STATE, SSM_GROUP_W), F32),
               pltpu.VMEM((SSM_GROUPS, SSM_HEADS_PER_GROUP * L, SSM_GROUP_W), BF16),
               pltpu.VMEM((LANES, L), F32)]
    y_shape = jax.ShapeDtypeStruct((m, SSM_D_INNER), BF16)
    return pl.pallas_call(
        _ssd_kernel,
        grid=(bsz, nc),
        in_specs=operands(fwd) + operands(bwd) + [_resident((1, LANES)), _resident((1, LANES))],
        out_specs=[pl.BlockSpec((L, SSM_D_INNER), lambda b, c: (fwd(b, c), 0)),
                   pl.BlockSpec((L, SSM_D_INNER), lambda b, c: (bwd(b, c), 0))],
        out_shape=[y_shape, y_shape],
        scratch_shapes=scratch + scratch,
        compiler_params=_params("parallel", "arbitrary"),
        name="ssd_scan",
    )(xbc, xbc, xbc, dt_raw, xbc, xbc, xbc, dt_raw, dt_bias_row, alog_row)


def kernel(x, attn_w_in, attn_sink, attn_rpb, attn_w_out, ssm_w_in, ssm_conv_w, ssm_conv_b, ssm_dt_bias,
           ssm_A_log, ssm_D, ssm_norm_w, ssm_w_out, mlp_w1, mlp_w2, ln1_g, ln1_b, ln2_g, ln2_b):
    bsz, seq, _ = x.shape
    x2d = x.reshape(bsz * seq, D_MODEL)

    qa, ka, va, qb, kb, vb = _attn_in_proj(x2d, attn_w_in[0].astype(BF16), seq)
    oa = _win_attn(qa, ka, va, attn_sink[0], bsz, seq)
    ob = _na_attn(qb, kb, vb, attn_rpb[0], bsz, seq)
    x2d = _attn_tail(oa, ob, x2d, attn_w_out[0].astype(BF16), ln1_g[0], ln1_b[0],
                     mlp_w1[0].astype(BF16), mlp_w2[0].astype(BF16), ln2_g[0], ln2_b[0])

    n_dt = 2 * SSM_HEADS
    w_pad = jnp.pad(ssm_w_in[0].astype(BF16), ((0, 0), (0, LANES - n_dt)))
    z, xbc, dt_raw = _ssm_in_proj(x2d, w_pad, ssm_conv_w[0], ssm_conv_b[0], seq)
    lane_pad = lambda a: jnp.pad(a.reshape(1, n_dt).astype(F32), ((0, 0), (0, LANES - n_dt)))
    dt_bias_row = lane_pad(ssm_dt_bias[0])
    alog_row = lane_pad(ssm_A_log[0])
    y_f, y_b = _ssd_scan(xbc, dt_raw, dt_bias_row, alog_row, bsz, seq)
    x2d = _ssm_tail(y_f, y_b, xbc, z, ssm_D[0], ssm_norm_w[0], x2d, ssm_w_out[0].astype(BF16),
                    ln1_g[1], ln1_b[1], mlp_w1[1].astype(BF16), mlp_w2[1].astype(BF16),
                    ln2_g[1], ln2_b[1])
    return x2d.reshape(bsz, seq, D_MODEL)
```

```python
import functools

import numpy as np
import jax
import jax.numpy as jnp
from jax import lax
from jax.experimental import pallas as pl
from jax.experimental.pallas import tpu as pltpu

F32 = jnp.float32
BF16 = jnp.bfloat16

D_MODEL = 1024
HEAD_DIM = 64
A_Q_HEADS = 8
A_KV_HEADS = 2
A_BLOCK = 128
WIN_QBLOCKS = 4
ROPE_THETA = 10000.0
B_HEADS = 8
GRID_W = 64
NA_KH = 8
NA_KW = 16
NA_QROWS = 4
NA_KROWS = 12
A_Q_DIM = A_Q_HEADS * HEAD_DIM
A_KV_DIM = A_KV_HEADS * HEAD_DIM
B_DIM = B_HEADS * HEAD_DIM
SSM_D_INNER = 2 * D_MODEL
SSM_HEADS = 32
SSM_GROUPS = 8
SSM_STATE = 128
SSM_CONV = 5
SSM_CHUNK = 128
SSM_GROUP_W = SSM_D_INNER // SSM_GROUPS
SSM_HEADS_PER_GROUP = SSM_HEADS // SSM_GROUPS
SSM_CONV_DIM = SSM_D_INNER + 2 * SSM_GROUPS * SSM_STATE
D_FF = 4 * D_MODEL
DEPTH = 2
DN_ALPHA = (2 * DEPTH) ** 0.25
LN_EPS = 1e-5
RMS_EPS = 1e-5
NEG_INF = -1e30
LOG2E = 1.4426950408889634
LANES = 128
HALO = 8

VMEM_LIMIT = 56 * 1024 * 1024


def _params(*sem):
    return pltpu.CompilerParams(dimension_semantics=sem, vmem_limit_bytes=VMEM_LIMIT)


def _resident(shape):
    nd = len(shape)
    return pl.BlockSpec(shape, lambda *_: (0,) * nd, pipeline_mode=pl.Buffered(1))


def _layer_norm(v, g, b):
    mu = jnp.mean(v, axis=-1, keepdims=True)
    d = v - mu
    var = jnp.mean(d * d, axis=-1, keepdims=True)
    return d * lax.rsqrt(var + LN_EPS) * g + b


def _silu(v):
    return v / (1.0 + jnp.exp2(v * (-LOG2E)))


def _attn_in_kernel(x_ref, w_ref, cos_ref, sin_ref, qa_ref, ka_ref, va_ref, qb_ref, kb_ref, vb_ref):
    tm = x_ref.shape[0]
    xb = x_ref[...].astype(BF16)
    cos = cos_ref[...]
    sin = sin_ref[...]
    lane = lax.broadcasted_iota(jnp.int32, (tm, LANES), 1)
    first_half = (lane % HEAD_DIM) < (HEAD_DIM // 2)
    scale = HEAD_DIM ** -0.5

    def rope(h):
        partner = jnp.where(first_half, pltpu.roll(h, LANES - HEAD_DIM // 2, 1),
                            pltpu.roll(h, HEAD_DIM // 2, 1))
        return h * cos + partner * sin

    n_rope = A_Q_DIM + A_KV_DIM
    h = jnp.dot(xb, w_ref[:, :n_rope], preferred_element_type=F32)
    for j in range(A_Q_DIM // LANES):
        qa_ref[:, j * LANES:(j + 1) * LANES] = (rope(h[:, j * LANES:(j + 1) * LANES]) * scale).astype(BF16)
    for j in range(A_KV_DIM // LANES):
        c0 = A_Q_DIM + j * LANES
        ka_ref[j * LANES:(j + 1) * LANES, :] = rope(h[:, c0:c0 + LANES]).T.astype(BF16)
    c0 = n_rope
    va_ref[...] = jnp.dot(xb, w_ref[:, c0:c0 + A_KV_DIM], preferred_element_type=F32).astype(BF16)
    c0 += A_KV_DIM
    qb_ref[...] = (jnp.dot(xb, w_ref[:, c0:c0 + B_DIM], preferred_element_type=F32) * scale).astype(BF16)
    c0 += B_DIM
    kb_ref[...] = jnp.dot(xb, w_ref[:, c0:c0 + B_DIM], preferred_element_type=F32).astype(BF16).T
    c0 += B_DIM
    vb_ref[...] = jnp.dot(xb, w_ref[:, c0:c0 + B_DIM], preferred_element_type=F32).astype(BF16)


def _attn_in_proj(x2d, w_bf16, seq):
    m = x2d.shape[0]
    tm = 1024
    half = HEAD_DIM // 2
    inv = ROPE_THETA ** (-jnp.arange(half, dtype=F32) / half)
    ang = jnp.arange(seq, dtype=F32)[:, None] * inv[None, :]
    reps = LANES // half
    cos_t = jnp.tile(jnp.cos(ang), (1, reps))
    sign = jnp.tile(jnp.concatenate([-jnp.ones((half,), F32), jnp.ones((half,), F32)]), LANES // HEAD_DIM)
    sin_t = jnp.tile(jnp.sin(ang), (1, reps)) * sign[None, :]
    tpb = seq // tm
    row = lambda i: (i, 0)
    pos = lambda i: (i % tpb, 0)
    col = lambda i: (0, i)
    tok = lambda n: (pl.BlockSpec((tm, n), row), jax.ShapeDtypeStruct((m, n), BF16))
    chan = lambda n: (pl.BlockSpec((n, tm), col), jax.ShapeDtypeStruct((n, m), BF16))
    outs = [tok(A_Q_DIM), chan(A_KV_DIM), tok(A_KV_DIM), tok(B_DIM), chan(B_DIM), tok(B_DIM)]
    return pl.pallas_call(
        _attn_in_kernel,
        grid=(m // tm,),
        in_specs=[pl.BlockSpec((tm, D_MODEL), row), _resident(w_bf16.shape),
                  pl.BlockSpec((tm, LANES), pos), pl.BlockSpec((tm, LANES), pos)],
        out_specs=[o[0] for o in outs],
        out_shape=[o[1] for o in outs],
        compiler_params=_params("parallel"),
        name="attn_in_proj",
    )(x2d, w_bf16, cos_t, sin_t)


def _win_attn_kernel(sink_ref, q_ref, kp_ref, kc_ref, kn_ref, vp_ref, vc_ref, vn_ref, o_ref):
    i = pl.program_id(1)
    ni = pl.num_programs(1)
    blk = A_BLOCK
    rep = A_Q_HEADS // A_KV_HEADS
    qi = lax.broadcasted_iota(jnp.int32, (rep * blk, 3 * blk), 0) % blk
    ki = lax.broadcasted_iota(jnp.int32, (rep * blk, 3 * blk), 1)
    band = jnp.abs(qi + blk - ki) <= blk
    kt_all = jnp.concatenate([kp_ref[...], kc_ref[...], kn_ref[...]], axis=1)
    v_all = jnp.concatenate([vp_ref[...], vc_ref[...], vn_ref[...]], axis=0)
    for sb in range(WIN_QBLOCKS):
        valid = band
        if sb == 0:
            valid = valid & ((ki >= blk) | (i > 0))
        if sb == WIN_QBLOCKS - 1:
            valid = valid & ((ki < 2 * blk) | (i < ni - 1))
        rows = slice(sb * blk, (sb + 1) * blk)
        for g in range(A_KV_HEADS):
            kg_t = kt_all[g * HEAD_DIM:(g + 1) * HEAD_DIM, sb * blk:(sb + 3) * blk]
            vg = v_all[sb * blk:(sb + 3) * blk, g * HEAD_DIM:(g + 1) * HEAD_DIM]
            heads = [g * rep + r for r in range(rep)]
            q4 = jnp.concatenate([q_ref[rows, h * HEAD_DIM:(h + 1) * HEAD_DIM] for h in heads], axis=0)
            s = jnp.dot(q4, kg_t, preferred_element_type=F32)
            s = jnp.where(valid, s, NEG_INF)
            for r, h in enumerate(heads):
                sr = s[r * blk:(r + 1) * blk]
                sink = sink_ref[h]
                mx = jnp.maximum(jnp.max(sr, axis=-1, keepdims=True), sink)
                p = jnp.exp(sr - mx)
                den = jnp.sum(p, axis=-1, keepdims=True) + jnp.exp(sink - mx)
                o = jnp.dot(p.astype(BF16), vg, preferred_element_type=F32) / den
                o_ref[rows, h * HEAD_DIM:(h + 1) * HEAD_DIM] = o.astype(BF16)


def _win_attn(qa, ka, va, sink, bsz, seq):
    nb = seq // A_BLOCK
    tq = WIN_QBLOCKS * A_BLOCK
    ni = seq // tq
    cur = lambda b, i: (b * ni + i, 0)
    prv = lambda b, i: (b * nb + jnp.maximum(i * WIN_QBLOCKS - 1, 0), 0)
    nxt = lambda b, i: (b * nb + jnp.minimum((i + 1) * WIN_QBLOCKS, nb - 1), 0)
    edge = lambda im: pl.BlockSpec((A_BLOCK, A_KV_DIM), im)
    body = pl.BlockSpec((tq, A_KV_DIM), cur)
    flip = lambda im: (lambda b, i: im(b, i)[::-1])
    edge_t = lambda im: pl.BlockSpec((A_KV_DIM, A_BLOCK), flip(im))
    body_t = pl.BlockSpec((A_KV_DIM, tq), flip(cur))
    return pl.pallas_call(
        _win_attn_kernel,
        grid=(bsz, ni),
        in_specs=[pl.BlockSpec(memory_space=pltpu.SMEM),
                  pl.BlockSpec((tq, A_Q_DIM), cur),
                  edge_t(prv), body_t, edge_t(nxt), edge(prv), body, edge(nxt)],
        out_specs=pl.BlockSpec((tq, A_Q_DIM), cur),
        out_shape=jax.ShapeDtypeStruct(qa.shape, BF16),
        compiler_params=_params("parallel", "parallel"),
        name="window_attn",
    )(sink.astype(F32), qa, ka, ka, ka, va, va, va)


def _na_row_tables(rows):
    nj = rows // NA_QROWS
    assert rows >= NA_KROWS + NA_QROWS and rows % NA_QROWS == 0

    def tile(j):
        ws = min(max(NA_QROWS * j - NA_KH // 2, 0), rows - NA_KROWS)
        r = NA_QROWS * j + np.arange(NA_QROWS)[:, None]
        rk = ws + np.arange(NA_KROWS)[None, :]
        rs = np.clip(r - NA_KH // 2, 0, rows - NA_KH)
        return (rk >= rs) & (rk < rs + NA_KH), rk - r + NA_KH - 1

    cases = [tile(0), tile(1), tile(nj - 1)]
    for j in range(1, nj - 1):
        ok, dr = tile(j)
        assert np.array_equal(ok, cases[1][0]) and np.array_equal(np.where(ok, dr, 0), np.where(ok, cases[1][1], 0))
    return np.stack([c[0] for c in cases]), np.stack([c[1] for c in cases])


def _na_bias(rpb, rows):
    ok_r, dr = _na_row_tables(rows)
    w = GRID_W
    pad = w - NA_KW
    padded = jnp.pad(rpb.astype(F32), ((0, 0), (0, 0), (pad, pad)))
    toep = jnp.stack([padded[:, :, w - 1 - cq:2 * w - 1 - cq] for cq in range(w)], axis=2)
    cq = np.arange(w)[:, None]
    ck = np.arange(w)[None, :]
    cs = np.clip(cq - NA_KW // 2, 0, w - NA_KW)
    toep = jnp.where((ck >= cs) & (ck < cs + NA_KW), toep, NEG_INF)
    outside = jnp.full((B_HEADS, w, w), NEG_INF, F32)
    tiles = []
    for case in range(ok_r.shape[0]):
        q_rows = []
        for rq in range(NA_QROWS):
            blocks = [toep[:, int(dr[case, rq, rk])] if ok_r[case, rq, rk] else outside
                      for rk in range(NA_KROWS)]
            q_rows.append(jnp.concatenate(blocks, axis=-1))
        tiles.append(jnp.concatenate(q_rows, axis=1))
    return jnp.stack(tiles)


def _na_kernel(q_ref, k0_ref, k1_ref, k2_ref, v0_ref, v1_ref, v2_ref, bias_ref, o_ref):
    kt_all = jnp.concatenate([k0_ref[...], k1_ref[...], k2_ref[...]], axis=1)
    v_all = jnp.concatenate([v0_ref[...], v1_ref[...], v2_ref[...]], axis=0)
    for h in range(B_HEADS):
        sl = slice(h * HEAD_DIM, (h + 1) * HEAD_DIM)
        s = jnp.dot(q_ref[:, sl], kt_all[sl, :], preferred_element_type=F32)
        s = s + bias_ref[0, h]
        mx = jnp.max(s, axis=-1, keepdims=True)
        p = jnp.exp(s - mx)
        den = jnp.sum(p, axis=-1, keepdims=True)
        o = jnp.dot(p.astype(BF16), v_all[:, sl], preferred_element_type=F32) / den
        o_ref[:, sl] = o.astype(BF16)


def _na_attn(qb, kb, vb, rpb, bsz, seq):
    rows = seq // GRID_W
    nj = rows // NA_QROWS
    bias = _na_bias(rpb, rows)
    tq = NA_QROWS * GRID_W
    kblocks = NA_KROWS // NA_QROWS
    per_b = seq // tq
    qmap = lambda b, j: (b * per_b + j, 0)

    def kmap(i):
        return lambda b, j: (b * per_b + jnp.clip(j - 1, 0, nj - kblocks) + i, 0)

    case = lambda b, j: (jnp.where(j == 0, 0, jnp.where(j == nj - 1, 2, 1)), 0, 0, 0)
    kv = [pl.BlockSpec((tq, B_DIM), kmap(i)) for i in range(kblocks)]
    kv_t = [pl.BlockSpec((B_DIM, tq), (lambda f: (lambda b, j: f(b, j)[::-1]))(kmap(i))) for i in range(kblocks)]
    return pl.pallas_call(
        _na_kernel,
        grid=(bsz, nj),
        in_specs=[pl.BlockSpec((tq, B_DIM), qmap)] + kv_t + kv
                 + [pl.BlockSpec((1, B_HEADS, tq, NA_KROWS * GRID_W), case)],
        out_specs=pl.BlockSpec((tq, B_DIM), qmap),
        out_shape=jax.ShapeDtypeStruct(qb.shape, BF16),
        compiler_params=_params("parallel", "arbitrary"),
        name="neighborhood_attn",
    )(qb, kb, kb, kb, vb, vb, vb, bias)


def _post_ln_mlp(x, mix, g1, b1, w1_ref, w2_ref, g2, b2):
    x1 = _layer_norm(DN_ALPHA * x + mix, g1, b1)
    xb = x1.astype(BF16)
    acc = jnp.zeros_like(x1)
    fc = 1024
    for c in range(D_FF // fc):
        h = jnp.dot(xb, w1_ref[:, c * fc:(c + 1) * fc], preferred_element_type=F32)
        h = jnp.maximum(h, 0.0)
        acc = acc + jnp.dot((h * h).astype(BF16), w2_ref[c * fc:(c + 1) * fc, :],
                            preferred_element_type=F32)
    return _layer_norm(DN_ALPHA * x1 + acc, g2, b2)


def _attn_tail_kernel(oa_ref, ob_ref, x_ref, wo_ref, g1_ref, b1_ref, w1_ref, w2_ref, g2_ref, b2_ref,
                      out_ref):
    mix = (jnp.dot(oa_ref[...], wo_ref[:A_Q_DIM, :], preferred_element_type=F32)
           + jnp.dot(ob_ref[...], wo_ref[A_Q_DIM:, :], preferred_element_type=F32))
    out_ref[...] = _post_ln_mlp(x_ref[...], mix, g1_ref[...], b1_ref[...], w1_ref, w2_ref,
                                g2_ref[...], b2_ref[...])


def _attn_tail(oa, ob, x2d, wo, g1, b1, w1, w2, g2, b2):
    m = x2d.shape[0]
    tm = 1024
    row = lambda i: (i, 0)
    vec = lambda a: a.reshape(1, -1).astype(F32)
    return pl.pallas_call(
        _attn_tail_kernel,
        grid=(m // tm,),
        in_specs=[pl.BlockSpec((tm, A_Q_DIM), row), pl.BlockSpec((tm, B_DIM), row),
                  pl.BlockSpec((tm, D_MODEL), row), _resident(wo.shape),
                  _resident((1, D_MODEL)), _resident((1, D_MODEL)),
                  _resident(w1.shape), _resident(w2.shape),
                  _resident((1, D_MODEL)), _resident((1, D_MODEL))],
        out_specs=pl.BlockSpec((tm, D_MODEL), row),
        out_shape=jax.ShapeDtypeStruct((m, D_MODEL), F32),
        compiler_params=_params("parallel"),
        name="attn_out_mlp",
    )(oa, ob, x2d, wo, vec(g1), vec(b1), w1, w2, vec(g2), vec(b2))


def _ssm_tail_kernel(yf_ref, yb_ref, xs_ref, z_ref, dskip_ref, nw_ref, x_ref, wo_ref, g1_ref, b1_ref,
                     w1_ref, w2_ref, g2_ref, b2_ref, out_ref):
    y = yf_ref[...].astype(F32) + yb_ref[...].astype(F32) + xs_ref[...].astype(F32) * dskip_ref[...]
    y = y * _silu(z_ref[...].astype(F32))
    parts = []
    for g in range(SSM_GROUPS):
        yg = y[:, g * SSM_GROUP_W:(g + 1) * SSM_GROUP_W]
        ms = jnp.mean(yg * yg, axis=-1, keepdims=True)
        parts.append(yg * lax.rsqrt(ms + RMS_EPS))
    yn = (jnp.concatenate(parts, axis=-1) * nw_ref[...]).astype(BF16)
    mix = jnp.dot(yn, wo_ref[...], preferred_element_type=F32)
    out_ref[...] = _post_ln_mlp(x_ref[...], mix, g1_ref[...], b1_ref[...], w1_ref, w2_ref,
                                g2_ref[...], b2_ref[...])


def _ssm_tail(yf, yb, xbc, z, d_skip, norm_w, x2d, wo, g1, b1, w1, w2, g2, b2):
    m = x2d.shape[0]
    tm = 512
    row = lambda i: (i, 0)
    vec = lambda a: a.reshape(1, -1).astype(F32)
    wide = pl.BlockSpec((tm, SSM_D_INNER), row)
    dskip_e = jnp.repeat(d_skip.astype(F32), SSM_D_INNER // SSM_HEADS).reshape(1, -1)
    return pl.pallas_call(
        _ssm_tail_kernel,
        grid=(m // tm,),
        in_specs=[wide, wide, wide, wide, _resident((1, SSM_D_INNER)), _resident((1, SSM_D_INNER)),
                  pl.BlockSpec((tm, D_MODEL), row), _resident(wo.shape),
                  _resident((1, D_MODEL)), _resident((1, D_MODEL)),
                  _resident(w1.shape), _resident(w2.shape),
                  _resident((1, D_MODEL)), _resident((1, D_MODEL))],
        out_specs=pl.BlockSpec((tm, D_MODEL), row),
        out_shape=jax.ShapeDtypeStruct((m, D_MODEL), F32),
        compiler_params=_params("parallel"),
        name="ssm_out_mlp",
    )(yf, yb, xbc, z, dskip_e, vec(norm_w), x2d, wo, vec(g1), vec(b1), w1, w2, vec(g2), vec(b2))


def _ssm_in_kernel(tiles_per_seq, prev_ref, x_ref, next_ref, w_ref, cw_ref, cb_ref,
                   z_ref, xs_ref, bt_ref, c_ref, dt_ref, raw_ref, act_ref):
    tm = x_ref.shape[0]
    pos = pl.program_id(0) % tiles_per_seq
    x = x_ref[...]
    xb = x.astype(BF16)
    nc = 1024
    for c in range(SSM_D_INNER // nc):
        z_ref[:, c * nc:(c + 1) * nc] = jnp.dot(
            xb, w_ref[:, c * nc:(c + 1) * nc], preferred_element_type=F32).astype(BF16)
    c0 = SSM_D_INNER + SSM_CONV_DIM
    dt_ref[...] = jnp.dot(xb, w_ref[:, c0:c0 + LANES], preferred_element_type=F32)
    prev = jnp.where(pos > 0, prev_ref[...], 0.0)
    nxt = jnp.where(pos < tiles_per_seq - 1, next_ref[...], 0.0)
    xe = jnp.concatenate([prev, x, nxt], axis=0).astype(BF16)
    pad = SSM_CONV // 2
    groups = tm // HALO
    for c in range(SSM_CONV_DIM // nc):
        cols = slice(c * nc, (c + 1) * nc)
        raw = jnp.dot(xe, w_ref[:, SSM_D_INNER + c * nc:SSM_D_INNER + (c + 1) * nc],
                      preferred_element_type=F32)
        for j in range(nc // LANES):
            lanes = slice(c * nc + j * LANES, c * nc + (j + 1) * LANES)
            raw_ref[j] = raw[:, j * LANES:(j + 1) * LANES]
            slabs = {}
            for ph in range(HALO):
                acc = cb_ref[:, lanes]
                for k in range(SSM_CONV):
                    m = HALO + ph + k - pad
                    if m not in slabs:
                        slabs[m] = raw_ref[j, pl.ds(m, groups, stride=HALO), :]
                    acc = acc + slabs[m] * cw_ref[k:k + 1, lanes]
                act_ref[j, pl.ds(ph, groups, stride=HALO), :] = _silu(acc)
            act = act_ref[j].astype(BF16)
            lo = c * nc + j * LANES
            if lo < SSM_D_INNER:
                xs_ref[:, lo:lo + LANES] = act
            elif lo < SSM_D_INNER + SSM_GROUPS * SSM_STATE:
                lo -= SSM_D_INNER
                bt_ref[lo:lo + LANES, :] = act.T
            else:
                lo -= SSM_D_INNER + SSM_GROUPS * SSM_STATE
                c_ref[:, lo:lo + LANES] = act


def _ssm_in_proj(x2d, w_pad, conv_w, conv_b, seq):
    m = x2d.shape[0]
    tm = 512
    tpb = seq // tm
    hb = tm // HALO
    last = m // HALO - 1
    n_bc = SSM_GROUPS * SSM_STATE
    row = lambda i: (i, 0)
    return pl.pallas_call(
        functools.partial(_ssm_in_kernel, tpb),
        grid=(m // tm,),
        in_specs=[pl.BlockSpec((HALO, D_MODEL), lambda i: (jnp.maximum(i * hb - 1, 0), 0)),
                  pl.BlockSpec((tm, D_MODEL), row),
                  pl.BlockSpec((HALO, D_MODEL), lambda i: (jnp.minimum((i + 1) * hb, last), 0)),
                  _resident(w_pad.shape), _resident((SSM_CONV, SSM_CONV_DIM)), _resident((1, SSM_CONV_DIM))],
        out_specs=[pl.BlockSpec((tm, SSM_D_INNER), row), pl.BlockSpec((tm, SSM_D_INNER), row),
                   pl.BlockSpec((n_bc, tm), lambda i: (0, i)), pl.BlockSpec((tm, n_bc), row),
                   pl.BlockSpec((tm, LANES), row)],
        out_shape=[jax.ShapeDtypeStruct((m, SSM_D_INNER), BF16),
                   jax.ShapeDtypeStruct((m, SSM_D_INNER), BF16),
                   jax.ShapeDtypeStruct((n_bc, m), BF16),
                   jax.ShapeDtypeStruct((m, n_bc), BF16),
                   jax.ShapeDtypeStruct((m, LANES), F32)],
        scratch_shapes=[pltpu.VMEM((1024 // LANES, tm + 2 * HALO, LANES), F32),
                        pltpu.VMEM((1024 // LANES, tm, LANES), F32)],
        compiler_params=_params("parallel"),
        name="ssm_in_conv",
    )(x2d, x2d, x2d, w_pad, conv_w.astype(F32), conv_b.reshape(1, -1).astype(F32))


def _ssd_chunk(reverse, xs_ref, bt_ref, c_ref, dtraw_ref, dtb_ref, alog_ref, y_ref,
               state_ref, blk_ref, rt_ref):
    L = SSM_CHUNK
    hpg = SSM_HEADS_PER_GROUP
    hw = SSM_GROUP_W // hpg
    off = SSM_HEADS if reverse else 0

    v = dtraw_ref[...] + dtb_ref[...]
    dt = jnp.maximum(v, 0.0) + jnp.log1p(jnp.exp(-jnp.abs(v)))
    a2 = dt * (-jnp.exp(alog_ref[...])) * LOG2E
    li = lax.broadcasted_iota(jnp.int32, (L, L), 0)
    si = lax.broadcasted_iota(jnp.int32, (L, L), 1)
    ac2 = a2
    s = 1
    while s < L:
        if reverse:
            ac2 = ac2 + jnp.where(li < L - s, pltpu.roll(ac2, L - s, 0), 0.0)
        else:
            ac2 = ac2 + jnp.where(li >= s, pltpu.roll(ac2, s, 0), 0.0)
        s *= 2
    tot2 = ac2[0:1, :] if reverse else ac2[L - 1:L, :]
    ldt = jnp.log2(dt)
    rt_ref[...] = (ac2 - ldt).T
    from_start = jnp.exp2(ac2)
    to_end = jnp.exp2(tot2 - ac2 + ldt)
    chunk_decay = jnp.exp2(tot2)
    tri = (si >= li) if reverse else (li >= si)

    lane = lax.broadcasted_iota(jnp.int32, (L, LANES), 1)

    def expand(src, g):
        if src.shape[0] != L:
            return jnp.concatenate(
                [jnp.broadcast_to(src[:, off + g * hpg + r:off + g * hpg + r + 1], (src.shape[0], hw))
                 for r in range(hpg)], axis=1)
        return jnp.concatenate(
            [jnp.take_along_axis(src, off + g * hpg + 2 * j + lane // hw, axis=1, mode="promise_in_bounds")
             for j in range(hpg // 2)], axis=1)

    for g in range(SSM_GROUPS):
        bg_t = bt_ref[g * SSM_STATE:(g + 1) * SSM_STATE, :]
        cg = c_ref[:, g * SSM_STATE:(g + 1) * SSM_STATE]
        cb = jnp.dot(cg, bg_t, preferred_element_type=F32)
        ms = []
        for r in range(hpg):
            h = off + g * hpg + r
            col = jnp.broadcast_to(ac2[:, h:h + 1], (L, L))
            lm = jnp.exp2(jnp.where(tri, col - rt_ref[h:h + 1, :], NEG_INF))
            ms.append((cb * lm).astype(BF16))
            c0 = g * SSM_GROUP_W + r * hw
            blk_ref[g, r * L:(r + 1) * L, r * hw:(r + 1) * hw] = xs_ref[:, c0:c0 + hw]
        prev = state_ref[g]
        y = jnp.dot(jnp.concatenate(ms, axis=1), blk_ref[g], preferred_element_type=F32)
        y = y + jnp.dot(cg, prev.astype(BF16), preferred_element_type=F32) * expand(from_start, g)
        y_ref[:, g * SSM_GROUP_W:(g + 1) * SSM_GROUP_W] = y.astype(y_ref.dtype)
        xg = xs_ref[:, g * SSM_GROUP_W:(g + 1) * SSM_GROUP_W].astype(F32)
        xd = (xg * expand(to_end, g)).astype(BF16)
        contrib = jnp.dot(bg_t, xd, preferred_element_type=F32)
        state_ref[g] = prev * expand(chunk_decay, g) + contrib


def _ssd_kernel(xf_ref, bf_ref, cf_ref, dtf_ref, xb_ref, bb_ref, cb_ref, dtb_ref, bias_ref, alog_ref,
                yf_ref, yb_ref, sf_ref, kf_ref, rf_ref, sb_ref, kb_ref, rb_ref):
    @pl.when(pl.program_id(1) == 0)
    def _():
        for ref in (sf_ref, kf_ref, sb_ref, kb_ref):
            ref[...] = jnp.zeros_like(ref)

    _ssd_chunk(False, xf_ref, bf_ref, cf_ref, dtf_ref, bias_ref, alog_ref, yf_ref, sf_ref, kf_ref, rf_ref)
    _ssd_chunk(True, xb_ref, bb_ref, cb_ref, dtb_ref, bias_ref, alog_ref, yb_ref, sb_ref, kb_ref, rb_ref)


def _ssd_scan(xs, b_t, cm, dt_raw, dt_bias_row, alog_row, bsz, seq):
    m = xs.shape[0]
    nc = seq // SSM_CHUNK
    L = SSM_CHUNK
    n_bc = SSM_GROUPS * SSM_STATE
    fwd = lambda b, c: b * nc + c
    bwd = lambda b, c: b * nc + nc - 1 - c

    def operands(ch):
        return [pl.BlockSpec((L, SSM_D_INNER), lambda b, c: (ch(b, c), 0)),
                pl.BlockSpec((n_bc, L), lambda b, c: (0, ch(b, c))),
                pl.BlockSpec((L, n_bc), lambda b, c: (ch(b, c), 0)),
                pl.BlockSpec((L, LANES), lambda b, c: (ch(b, c), 0))]

    scratch = [pltpu.VMEM((SSM_GROUPS, SSM_STATE, SSM_GROUP_W), F32),
               pltpu.VMEM((SSM_GROUPS, SSM_HEADS_PER_GROUP * L, SSM_GROUP_W), BF16),
               pltpu.VMEM((LANES, L), F32)]
    y_shape = jax.ShapeDtypeStruct((m, SSM_D_INNER), BF16)
    return pl.pallas_call(
        _ssd_kernel,
        grid=(bsz, nc),
        in_specs=operands(fwd) + operands(bwd) + [_resident((1, LANES)), _resident((1, LANES))],
        out_specs=[pl.BlockSpec((L, SSM_D_INNER), lambda b, c: (fwd(b, c), 0)),
                   pl.BlockSpec((L, SSM_D_INNER), lambda b, c: (bwd(b, c), 0))],
        out_shape=[y_shape, y_shape],
        scratch_shapes=scratch + scratch,
        compiler_params=_params("parallel", "arbitrary"),
        name="ssd_scan",
    )(xs, b_t, cm, dt_raw, xs, b_t, cm, dt_raw, dt_bias_row, alog_row)


def kernel(x, attn_w_in, attn_sink, attn_rpb, attn_w_out, ssm_w_in, ssm_conv_w, ssm_conv_b, ssm_dt_bias,
           ssm_A_log, ssm_D, ssm_norm_w, ssm_w_out, mlp_w1, mlp_w2, ln1_g, ln1_b, ln2_g, ln2_b):
    bsz, seq, _ = x.shape
    x2d = x.reshape(bsz * seq, D_MODEL)

    qa, ka, va, qb, kb, vb = _attn_in_proj(x2d, attn_w_in[0].astype(BF16), seq)
    oa = _win_attn(qa, ka, va, attn_sink[0], bsz, seq)
    ob = _na_attn(qb, kb, vb, attn_rpb[0], bsz, seq)
    x2d = _attn_tail(oa, ob, x2d, attn_w_out[0].astype(BF16), ln1_g[0], ln1_b[0],
                     mlp_w1[0].astype(BF16), mlp_w2[0].astype(BF16), ln2_g[0], ln2_b[0])

    n_dt = 2 * SSM_HEADS
    w_pad = jnp.pad(ssm_w_in[0].astype(BF16), ((0, 0), (0, LANES - n_dt)))
    z, xs, b_t, cm, dt_raw = _ssm_in_proj(x2d, w_pad, ssm_conv_w[0], ssm_conv_b[0], seq)
    lane_pad = lambda a: jnp.pad(a.reshape(1, n_dt).astype(F32), ((0, 0), (0, LANES - n_dt)))
    y_f, y_b = _ssd_scan(xs, b_t, cm, dt_raw, lane_pad(ssm_dt_bias[0]), lane_pad(ssm_A_log[0]), bsz, seq)
    x2d = _ssm_tail(y_f, y_b, xs, z, ssm_D[0], ssm_norm_w[0], x2d, ssm_w_out[0].astype(BF16),
                    ln1_g[1], ln1_b[1], mlp_w1[1].astype(BF16), mlp_w2[1].astype(BF16),
                    ln2_g[1], ln2_b[1])
    return x2d.reshape(bsz, seq, D_MODEL)
```

```python
import functools

import numpy as np
import jax
import jax.numpy as jnp
from jax import lax
from jax.experimental import pallas as pl
from jax.experimental.pallas import tpu as pltpu

F32 = jnp.float32
BF16 = jnp.bfloat16

D_MODEL = 1024
HEAD_DIM = 64
A_Q_HEADS = 8
A_KV_HEADS = 2
A_BLOCK = 128
WIN_QBLOCKS = 4
ROPE_THETA = 10000.0
B_HEADS = 8
GRID_W = 64
NA_KH = 8
NA_KW = 16
NA_QROWS = 4
NA_KROWS = 12
A_Q_DIM = A_Q_HEADS * HEAD_DIM
A_KV_DIM = A_KV_HEADS * HEAD_DIM
B_DIM = B_HEADS * HEAD_DIM
SSM_D_INNER = 2 * D_MODEL
SSM_HEADS = 32
SSM_GROUPS = 8
SSM_STATE = 128
SSM_CONV = 5
SSM_CHUNK = 128
SSM_GROUP_W = SSM_D_INNER // SSM_GROUPS
SSM_HEADS_PER_GROUP = SSM_HEADS // SSM_GROUPS
SSM_CONV_DIM = SSM_D_INNER + 2 * SSM_GROUPS * SSM_STATE
D_FF = 4 * D_MODEL
DEPTH = 2
DN_ALPHA = (2 * DEPTH) ** 0.25
LN_EPS = 1e-5
RMS_EPS = 1e-5
NEG_INF = -1e30
LOG2E = 1.4426950408889634
LANES = 128
HALO = 8
CONV_PHASES = 4

VMEM_LIMIT = 56 * 1024 * 1024


def _params(*sem):
    return pltpu.CompilerParams(dimension_semantics=sem, vmem_limit_bytes=VMEM_LIMIT)


def _resident(shape):
    nd = len(shape)
    return pl.BlockSpec(shape, lambda *_: (0,) * nd, pipeline_mode=pl.Buffered(1))


def _layer_norm(v, g, b):
    mu = jnp.mean(v, axis=-1, keepdims=True)
    d = v - mu
    var = jnp.mean(d * d, axis=-1, keepdims=True)
    return d * lax.rsqrt(var + LN_EPS) * g + b


def _silu(v):
    return v / (1.0 + jnp.exp2(v * (-LOG2E)))


def _attn_in_kernel(x_ref, w_ref, cos_ref, sin_ref, qa_ref, ka_ref, va_ref, qb_ref, kb_ref, vb_ref):
    tm = x_ref.shape[0]
    xb = x_ref[...].astype(BF16)
    cos = cos_ref[...]
    sin = sin_ref[...]
    lane = lax.broadcasted_iota(jnp.int32, (tm, LANES), 1)
    first_half = (lane % HEAD_DIM) < (HEAD_DIM // 2)
    scale = HEAD_DIM ** -0.5

    def rope(h):
        partner = jnp.where(first_half, pltpu.roll(h, LANES - HEAD_DIM // 2, 1),
                            pltpu.roll(h, HEAD_DIM // 2, 1))
        return h * cos + partner * sin

    n_rope = A_Q_DIM + A_KV_DIM
    h = jnp.dot(xb, w_ref[:, :n_rope], preferred_element_type=F32)
    for j in range(A_Q_DIM // LANES):
        qa_ref[:, j * LANES:(j + 1) * LANES] = (rope(h[:, j * LANES:(j + 1) * LANES]) * scale).astype(BF16)
    for j in range(A_KV_DIM // LANES):
        c0 = A_Q_DIM + j * LANES
        ka_ref[j * LANES:(j + 1) * LANES, :] = rope(h[:, c0:c0 + LANES]).T.astype(BF16)
    c0 = n_rope
    va_ref[...] = jnp.dot(xb, w_ref[:, c0:c0 + A_KV_DIM], preferred_element_type=F32).astype(BF16)
    c0 += A_KV_DIM
    qb_ref[...] = (jnp.dot(xb, w_ref[:, c0:c0 + B_DIM], preferred_element_type=F32) * scale).astype(BF16)
    c0 += B_DIM
    kb_ref[...] = jnp.dot(xb, w_ref[:, c0:c0 + B_DIM], preferred_element_type=F32).astype(BF16).T
    c0 += B_DIM
    vb_ref[...] = jnp.dot(xb, w_ref[:, c0:c0 + B_DIM], preferred_element_type=F32).astype(BF16)


def _attn_in_proj(x2d, w_bf16, seq):
    m = x2d.shape[0]
    tm = 1024
    half = HEAD_DIM // 2
    inv = ROPE_THETA ** (-jnp.arange(half, dtype=F32) / half)
    ang = jnp.arange(seq, dtype=F32)[:, None] * inv[None, :]
    reps = LANES // half
    cos_t = jnp.tile(jnp.cos(ang), (1, reps))
    sign = jnp.tile(jnp.concatenate([-jnp.ones((half,), F32), jnp.ones((half,), F32)]), LANES // HEAD_DIM)
    sin_t = jnp.tile(jnp.sin(ang), (1, reps)) * sign[None, :]
    tpb = seq // tm
    row = lambda i: (i, 0)
    pos = lambda i: (i % tpb, 0)
    col = lambda i: (0, i)
    tok = lambda n: (pl.BlockSpec((tm, n), row), jax.ShapeDtypeStruct((m, n), BF16))
    chan = lambda n: (pl.BlockSpec((n, tm), col), jax.ShapeDtypeStruct((n, m), BF16))
    outs = [tok(A_Q_DIM), chan(A_KV_DIM), tok(A_KV_DIM), tok(B_DIM), chan(B_DIM), tok(B_DIM)]
    return pl.pallas_call(
        _attn_in_kernel,
        grid=(m // tm,),
        in_specs=[pl.BlockSpec((tm, D_MODEL), row), _resident(w_bf16.shape),
                  pl.BlockSpec((tm, LANES), pos), pl.BlockSpec((tm, LANES), pos)],
        out_specs=[o[0] for o in outs],
        out_shape=[o[1] for o in outs],
        compiler_params=_params("parallel"),
        name="attn_in_proj",
    )(x2d, w_bf16, cos_t, sin_t)


def _win_attn_kernel(sink_ref, q_ref, kp_ref, kc_ref, kn_ref, vp_ref, vc_ref, vn_ref, o_ref):
    i = pl.program_id(1)
    ni = pl.num_programs(1)
    blk = A_BLOCK
    rep = A_Q_HEADS // A_KV_HEADS
    qi = lax.broadcasted_iota(jnp.int32, (rep * blk, 3 * blk), 0) % blk
    ki = lax.broadcasted_iota(jnp.int32, (rep * blk, 3 * blk), 1)
    band = jnp.abs(qi + blk - ki) <= blk
    kt_all = jnp.concatenate([kp_ref[...], kc_ref[...], kn_ref[...]], axis=1)
    v_all = jnp.concatenate([vp_ref[...], vc_ref[...], vn_ref[...]], axis=0)
    for sb in range(WIN_QBLOCKS):
        valid = band
        if sb == 0:
            valid = valid & ((ki >= blk) | (i > 0))
        if sb == WIN_QBLOCKS - 1:
            valid = valid & ((ki < 2 * blk) | (i < ni - 1))
        rows = slice(sb * blk, (sb + 1) * blk)
        for g in range(A_KV_HEADS):
            kg_t = kt_all[g * HEAD_DIM:(g + 1) * HEAD_DIM, sb * blk:(sb + 3) * blk]
            vg = v_all[sb * blk:(sb + 3) * blk, g * HEAD_DIM:(g + 1) * HEAD_DIM]
            heads = [g * rep + r for r in range(rep)]
            q4 = jnp.concatenate([q_ref[rows, h * HEAD_DIM:(h + 1) * HEAD_DIM] for h in heads], axis=0)
            s = jnp.dot(q4, kg_t, preferred_element_type=F32)
            s = jnp.where(valid, s, NEG_INF)
            for r, h in enumerate(heads):
                sr = s[r * blk:(r + 1) * blk]
                sink = sink_ref[h]
                mx = jnp.maximum(jnp.max(sr, axis=-1, keepdims=True), sink)
                p = jnp.exp(sr - mx)
                den = jnp.sum(p, axis=-1, keepdims=True) + jnp.exp(sink - mx)
                o = jnp.dot(p.astype(BF16), vg, preferred_element_type=F32) / den
                o_ref[rows, h * HEAD_DIM:(h + 1) * HEAD_DIM] = o.astype(BF16)


def _win_attn(qa, ka, va, sink, bsz, seq):
    nb = seq // A_BLOCK
    tq = WIN_QBLOCKS * A_BLOCK
    ni = seq // tq
    cur = lambda b, i: (b * ni + i, 0)
    prv = lambda b, i: (b * nb + jnp.maximum(i * WIN_QBLOCKS - 1, 0), 0)
    nxt = lambda b, i: (b * nb + jnp.minimum((i + 1) * WIN_QBLOCKS, nb - 1), 0)
    edge = lambda im: pl.BlockSpec((A_BLOCK, A_KV_DIM), im)
    body = pl.BlockSpec((tq, A_KV_DIM), cur)
    flip = lambda im: (lambda b, i: im(b, i)[::-1])
    edge_t = lambda im: pl.BlockSpec((A_KV_DIM, A_BLOCK), flip(im))
    body_t = pl.BlockSpec((A_KV_DIM, tq), flip(cur))
    return pl.pallas_call(
        _win_attn_kernel,
        grid=(bsz, ni),
        in_specs=[pl.BlockSpec(memory_space=pltpu.SMEM),
                  pl.BlockSpec((tq, A_Q_DIM), cur),
                  edge_t(prv), body_t, edge_t(nxt), edge(prv), body, edge(nxt)],
        out_specs=pl.BlockSpec((tq, A_Q_DIM), cur),
        out_shape=jax.ShapeDtypeStruct(qa.shape, BF16),
        compiler_params=_params("parallel", "parallel"),
        name="window_attn",
    )(sink.astype(F32), qa, ka, ka, ka, va, va, va)


def _na_row_tables(rows):
    nj = rows // NA_QROWS
    assert rows >= NA_KROWS + NA_QROWS and rows % NA_QROWS == 0

    def tile(j):
        ws = min(max(NA_QROWS * j - NA_KH // 2, 0), rows - NA_KROWS)
        r = NA_QROWS * j + np.arange(NA_QROWS)[:, None]
        rk = ws + np.arange(NA_KROWS)[None, :]
        rs = np.clip(r - NA_KH // 2, 0, rows - NA_KH)
        return (rk >= rs) & (rk < rs + NA_KH), rk - r + NA_KH - 1

    cases = [tile(0), tile(1), tile(nj - 1)]
    for j in range(1, nj - 1):
        ok, dr = tile(j)
        assert np.array_equal(ok, cases[1][0]) and np.array_equal(np.where(ok, dr, 0), np.where(ok, cases[1][1], 0))
    return np.stack([c[0] for c in cases]), np.stack([c[1] for c in cases])


def _na_bias(rpb, rows):
    ok_r, dr = _na_row_tables(rows)
    w = GRID_W
    pad = w - NA_KW
    padded = jnp.pad(rpb.astype(F32), ((0, 0), (0, 0), (pad, pad)))
    toep = jnp.stack([padded[:, :, w - 1 - cq:2 * w - 1 - cq] for cq in range(w)], axis=2)
    cq = np.arange(w)[:, None]
    ck = np.arange(w)[None, :]
    cs = np.clip(cq - NA_KW // 2, 0, w - NA_KW)
    toep = jnp.where((ck >= cs) & (ck < cs + NA_KW), toep, NEG_INF)
    outside = jnp.full((B_HEADS, w, w), NEG_INF, F32)
    tiles = []
    for case in range(ok_r.shape[0]):
        q_rows = []
        for rq in range(NA_QROWS):
            blocks = [toep[:, int(dr[case, rq, rk])] if ok_r[case, rq, rk] else outside
                      for rk in range(NA_KROWS)]
            q_rows.append(jnp.concatenate(blocks, axis=-1))
        tiles.append(jnp.concatenate(q_rows, axis=1))
    return jnp.stack(tiles)


def _na_kernel(q_ref, k0_ref, k1_ref, k2_ref, v0_ref, v1_ref, v2_ref, bias_ref, o_ref):
    kt_all = jnp.concatenate([k0_ref[...], k1_ref[...], k2_ref[...]], axis=1)
    v_all = jnp.concatenate([v0_ref[...], v1_ref[...], v2_ref[...]], axis=0)
    for h in range(B_HEADS):
        sl = slice(h * HEAD_DIM, (h + 1) * HEAD_DIM)
        s = jnp.dot(q_ref[:, sl], kt_all[sl, :], preferred_element_type=F32)
        s = s + bias_ref[0, h]
        mx = jnp.max(s, axis=-1, keepdims=True)
        p = jnp.exp(s - mx)
        den = jnp.sum(p, axis=-1, keepdims=True)
        o = jnp.dot(p.astype(BF16), v_all[:, sl], preferred_element_type=F32) / den
        o_ref[:, sl] = o.astype(BF16)


def _na_attn(qb, kb, vb, rpb, bsz, seq):
    rows = seq // GRID_W
    nj = rows // NA_QROWS
    bias = _na_bias(rpb, rows)
    tq = NA_QROWS * GRID_W
    kblocks = NA_KROWS // NA_QROWS
    per_b = seq // tq
    qmap = lambda b, j: (b * per_b + j, 0)

    def kmap(i):
        return lambda b, j: (b * per_b + jnp.clip(j - 1, 0, nj - kblocks) + i, 0)

    case = lambda b, j: (jnp.where(j == 0, 0, jnp.where(j == nj - 1, 2, 1)), 0, 0, 0)
    kv = [pl.BlockSpec((tq, B_DIM), kmap(i)) for i in range(kblocks)]
    kv_t = [pl.BlockSpec((B_DIM, tq), (lambda f: (lambda b, j: f(b, j)[::-1]))(kmap(i))) for i in range(kblocks)]
    return pl.pallas_call(
        _na_kernel,
        grid=(bsz, nj),
        in_specs=[pl.BlockSpec((tq, B_DIM), qmap)] + kv_t + kv
                 + [pl.BlockSpec((1, B_HEADS, tq, NA_KROWS * GRID_W), case)],
        out_specs=pl.BlockSpec((tq, B_DIM), qmap),
        out_shape=jax.ShapeDtypeStruct(qb.shape, BF16),
        compiler_params=_params("parallel", "arbitrary"),
        name="neighborhood_attn",
    )(qb, kb, kb, kb, vb, vb, vb, bias)


def _post_ln_mlp(x, mix, g1, b1, w1_ref, w2_ref, g2, b2):
    x1 = _layer_norm(DN_ALPHA * x + mix, g1, b1)
    xb = x1.astype(BF16)
    acc = jnp.zeros_like(x1)
    fc = 1024
    for c in range(D_FF // fc):
        h = jnp.dot(xb, w1_ref[:, c * fc:(c + 1) * fc], preferred_element_type=F32)
        h = jnp.maximum(h, 0.0)
        acc = acc + jnp.dot((h * h).astype(BF16), w2_ref[c * fc:(c + 1) * fc, :],
                            preferred_element_type=F32)
    return _layer_norm(DN_ALPHA * x1 + acc, g2, b2)


def _attn_tail_kernel(oa_ref, ob_ref, x_ref, wo_ref, g1_ref, b1_ref, w1_ref, w2_ref, g2_ref, b2_ref,
                      out_ref):
    mix = (jnp.dot(oa_ref[...], wo_ref[:A_Q_DIM, :], preferred_element_type=F32)
           + jnp.dot(ob_ref[...], wo_ref[A_Q_DIM:, :], preferred_element_type=F32))
    out_ref[...] = _post_ln_mlp(x_ref[...], mix, g1_ref[...], b1_ref[...], w1_ref, w2_ref,
                                g2_ref[...], b2_ref[...])


def _attn_tail(oa, ob, x2d, wo, g1, b1, w1, w2, g2, b2):
    m = x2d.shape[0]
    tm = 1024
    row = lambda i: (i, 0)
    vec = lambda a: a.reshape(1, -1).astype(F32)
    return pl.pallas_call(
        _attn_tail_kernel,
        grid=(m // tm,),
        in_specs=[pl.BlockSpec((tm, A_Q_DIM), row), pl.BlockSpec((tm, B_DIM), row),
                  pl.BlockSpec((tm, D_MODEL), row), _resident(wo.shape),
                  _resident((1, D_MODEL)), _resident((1, D_MODEL)),
                  _resident(w1.shape), _resident(w2.shape),
                  _resident((1, D_MODEL)), _resident((1, D_MODEL))],
        out_specs=pl.BlockSpec((tm, D_MODEL), row),
        out_shape=jax.ShapeDtypeStruct((m, D_MODEL), F32),
        compiler_params=_params("parallel"),
        name="attn_out_mlp",
    )(oa, ob, x2d, wo, vec(g1), vec(b1), w1, w2, vec(g2), vec(b2))


def _ssm_tail_kernel(yf_ref, yb_ref, xs_ref, z_ref, dskip_ref, nw_ref, x_ref, wo_ref, g1_ref, b1_ref,
                     w1_ref, w2_ref, g2_ref, b2_ref, out_ref):
    y = yf_ref[...].astype(F32) + yb_ref[...].astype(F32) + xs_ref[...].astype(F32) * dskip_ref[...]
    y = y * z_ref[...].astype(F32)
    parts = []
    for g in range(SSM_GROUPS):
        yg = y[:, g * SSM_GROUP_W:(g + 1) * SSM_GROUP_W]
        ms = jnp.mean(yg * yg, axis=-1, keepdims=True)
        parts.append(yg * lax.rsqrt(ms + RMS_EPS))
    yn = (jnp.concatenate(parts, axis=-1) * nw_ref[...]).astype(BF16)
    mix = jnp.dot(yn, wo_ref[...], preferred_element_type=F32)
    out_ref[...] = _post_ln_mlp(x_ref[...], mix, g1_ref[...], b1_ref[...], w1_ref, w2_ref,
                                g2_ref[...], b2_ref[...])


def _ssm_tail(yf, yb, xbc, z, d_skip, norm_w, x2d, wo, g1, b1, w1, w2, g2, b2):
    m = x2d.shape[0]
    tm = 512
    row = lambda i: (i, 0)
    vec = lambda a: a.reshape(1, -1).astype(F32)
    wide = pl.BlockSpec((tm, SSM_D_INNER), row)
    dskip_e = jnp.repeat(d_skip.astype(F32), SSM_D_INNER // SSM_HEADS).reshape(1, -1)
    return pl.pallas_call(
        _ssm_tail_kernel,
        grid=(m // tm,),
        in_specs=[wide, wide, wide, wide, _resident((1, SSM_D_INNER)), _resident((1, SSM_D_INNER)),
                  pl.BlockSpec((tm, D_MODEL), row), _resident(wo.shape),
                  _resident((1, D_MODEL)), _resident((1, D_MODEL)),
                  _resident(w1.shape), _resident(w2.shape),
                  _resident((1, D_MODEL)), _resident((1, D_MODEL))],
        out_specs=pl.BlockSpec((tm, D_MODEL), row),
        out_shape=jax.ShapeDtypeStruct((m, D_MODEL), F32),
        compiler_params=_params("parallel"),
        name="ssm_out_mlp",
    )(yf, yb, xbc, z, dskip_e, vec(norm_w), x2d, wo, vec(g1), vec(b1), w1, w2, vec(g2), vec(b2))


def _ssm_in_kernel(tiles_per_seq, prev_ref, x_ref, next_ref, w_ref, cw_ref, cb_ref,
                   z_ref, xs_ref, bt_ref, c_ref, dt_ref, raw_ref, act_ref):
    tm = x_ref.shape[0]
    pos = pl.program_id(0) % tiles_per_seq
    x = x_ref[...]
    xb = x.astype(BF16)
    nc = 1024
    for c in range(SSM_D_INNER // nc):
        z_ref[:, c * nc:(c + 1) * nc] = _silu(jnp.dot(
            xb, w_ref[:, c * nc:(c + 1) * nc], preferred_element_type=F32)).astype(BF16)
    c0 = SSM_D_INNER + SSM_CONV_DIM
    dt_ref[...] = jnp.dot(xb, w_ref[:, c0:c0 + LANES], preferred_element_type=F32)
    prev = jnp.where(pos > 0, prev_ref[...], 0.0)
    nxt = jnp.where(pos < tiles_per_seq - 1, next_ref[...], 0.0)
    xe = jnp.concatenate([prev, x, nxt], axis=0).astype(BF16)
    pad = SSM_CONV // 2
    groups = tm // CONV_PHASES
    for c in range(SSM_CONV_DIM // nc):
        cols = slice(c * nc, (c + 1) * nc)
        raw = jnp.dot(xe, w_ref[:, SSM_D_INNER + c * nc:SSM_D_INNER + (c + 1) * nc],
                      preferred_element_type=F32)
        for j in range(nc // LANES):
            lanes = slice(c * nc + j * LANES, c * nc + (j + 1) * LANES)
            raw_ref[j] = raw[:, j * LANES:(j + 1) * LANES]
            slabs = {}
            for ph in range(CONV_PHASES):
                acc = cb_ref[:, lanes]
                for k in range(SSM_CONV):
                    m = HALO + ph + k - pad
                    if m not in slabs:
                        slabs[m] = raw_ref[j, pl.ds(m, groups, stride=CONV_PHASES), :]
                    acc = acc + slabs[m] * cw_ref[k:k + 1, lanes]
                act_ref[j, pl.ds(ph, groups, stride=CONV_PHASES), :] = _silu(acc)
            act = act_ref[j].astype(BF16)
            lo = c * nc + j * LANES
            if lo < SSM_D_INNER:
                xs_ref[:, lo:lo + LANES] = act
            elif lo < SSM_D_INNER + SSM_GROUPS * SSM_STATE:
                lo -= SSM_D_INNER
                bt_ref[lo:lo + LANES, :] = act.T
            else:
                lo -= SSM_D_INNER + SSM_GROUPS * SSM_STATE
                c_ref[:, lo:lo + LANES] = act


def _ssm_in_proj(x2d, w_pad, conv_w, conv_b, seq):
    m = x2d.shape[0]
    tm = 512
    tpb = seq // tm
    hb = tm // HALO
    last = m // HALO - 1
    n_bc = SSM_GROUPS * SSM_STATE
    row = lambda i: (i, 0)
    return pl.pallas_call(
        functools.partial(_ssm_in_kernel, tpb),
        grid=(m // tm,),
        in_specs=[pl.BlockSpec((HALO, D_MODEL), lambda i: (jnp.maximum(i * hb - 1, 0), 0)),
                  pl.BlockSpec((tm, D_MODEL), row),
                  pl.BlockSpec((HALO, D_MODEL), lambda i: (jnp.minimum((i + 1) * hb, last), 0)),
                  _resident(w_pad.shape), _resident((SSM_CONV, SSM_CONV_DIM)), _resident((1, SSM_CONV_DIM))],
        out_specs=[pl.BlockSpec((tm, SSM_D_INNER), row), pl.BlockSpec((tm, SSM_D_INNER), row),
                   pl.BlockSpec((n_bc, tm), lambda i: (0, i)), pl.BlockSpec((tm, n_bc), row),
                   pl.BlockSpec((tm, LANES), row)],
        out_shape=[jax.ShapeDtypeStruct((m, SSM_D_INNER), BF16),
                   jax.ShapeDtypeStruct((m, SSM_D_INNER), BF16),
                   jax.ShapeDtypeStruct((n_bc, m), BF16),
                   jax.ShapeDtypeStruct((m, n_bc), BF16),
                   jax.ShapeDtypeStruct((m, LANES), F32)],
        scratch_shapes=[pltpu.VMEM((1024 // LANES, tm + 2 * HALO, LANES), F32),
                        pltpu.VMEM((1024 // LANES, tm, LANES), F32)],
        compiler_params=_params("parallel"),
        name="ssm_in_conv",
    )(x2d, x2d, x2d, w_pad, conv_w.astype(F32), conv_b.reshape(1, -1).astype(F32))


def _ssd_chunk(reverse, xs_ref, bt_ref, c_ref, dtraw_ref, dtb_ref, alog_ref, y_ref,
               state_ref, blk_ref, rt_ref):
    L = SSM_CHUNK
    hpg = SSM_HEADS_PER_GROUP
    hw = SSM_GROUP_W // hpg
    off = SSM_HEADS if reverse else 0

    v = dtraw_ref[...] + dtb_ref[...]
    dt = jnp.maximum(v, 0.0) + jnp.log1p(jnp.exp(-jnp.abs(v)))
    a2 = dt * (-jnp.exp(alog_ref[...])) * LOG2E
    li = lax.broadcasted_iota(jnp.int32, (L, L), 0)
    si = lax.broadcasted_iota(jnp.int32, (L, L), 1)
    ac2 = a2
    s = 1
    while s < L:
        if reverse:
            ac2 = ac2 + jnp.where(li < L - s, pltpu.roll(ac2, L - s, 0), 0.0)
        else:
            ac2 = ac2 + jnp.where(li >= s, pltpu.roll(ac2, s, 0), 0.0)
        s *= 2
    tot2 = ac2[0:1, :] if reverse else ac2[L - 1:L, :]
    ldt = jnp.log2(dt)
    rt_ref[...] = (ac2 - ldt).T
    from_start = jnp.exp2(ac2)
    to_end = jnp.exp2(tot2 - ac2 + ldt)
    chunk_decay = jnp.exp2(tot2)
    tri = (si >= li) if reverse else (li >= si)

    lane = lax.broadcasted_iota(jnp.int32, (L, LANES), 1)

    def expand(src, g):
        if src.shape[0] != L:
            return jnp.concatenate(
                [jnp.broadcast_to(src[:, off + g * hpg + r:off + g * hpg + r + 1], (src.shape[0], hw))
                 for r in range(hpg)], axis=1)
        return jnp.concatenate(
            [jnp.take_along_axis(src, off + g * hpg + 2 * j + lane // hw, axis=1, mode="promise_in_bounds")
             for j in range(hpg // 2)], axis=1)

    for g in range(SSM_GROUPS):
        bg_t = bt_ref[g * SSM_STATE:(g + 1) * SSM_STATE, :]
        cg = c_ref[:, g * SSM_STATE:(g + 1) * SSM_STATE]
        cb = jnp.dot(cg, bg_t, preferred_element_type=F32)
        ms = []
        for r in range(hpg):
            h = off + g * hpg + r
            col = jnp.broadcast_to(ac2[:, h:h + 1], (L, L))
            lm = jnp.exp2(jnp.where(tri, col - rt_ref[h:h + 1, :], NEG_INF))
            ms.append((cb * lm).astype(BF16))
            c0 = g * SSM_GROUP_W + r * hw
            blk_ref[g, r * L:(r + 1) * L, r * hw:(r + 1) * hw] = xs_ref[:, c0:c0 + hw]
        prev = state_ref[g]
        y = jnp.dot(jnp.concatenate(ms, axis=1), blk_ref[g], preferred_element_type=F32)
        y = y + jnp.dot(cg, prev.astype(BF16), preferred_element_type=F32) * expand(from_start, g)
        y_ref[:, g * SSM_GROUP_W:(g + 1) * SSM_GROUP_W] = y.astype(y_ref.dtype)
        xg = xs_ref[:, g * SSM_GROUP_W:(g + 1) * SSM_GROUP_W].astype(F32)
        xd = (xg * expand(to_end, g)).astype(BF16)
        contrib = jnp.dot(bg_t, xd, preferred_element_type=F32)
        state_ref[g] = prev * expand(chunk_decay, g) + contrib


def _ssd_kernel(xf_ref, bf_ref, cf_ref, dtf_ref, xb_ref, bb_ref, cb_ref, dtb_ref, bias_ref, alog_ref,
                yf_ref, yb_ref, sf_ref, kf_ref, rf_ref, sb_ref, kb_ref, rb_ref):
    @pl.when(pl.program_id(1) == 0)
    def _():
        for ref in (sf_ref, kf_ref, sb_ref, kb_ref):
            ref[...] = jnp.zeros_like(ref)

    _ssd_chunk(False, xf_ref, bf_ref, cf_ref, dtf_ref, bias_ref, alog_ref, yf_ref, sf_ref, kf_ref, rf_ref)
    _ssd_chunk(True, xb_ref, bb_ref, cb_ref, dtb_ref, bias_ref, alog_ref, yb_ref, sb_ref, kb_ref, rb_ref)


def _ssd_scan(xs, b_t, cm, dt_raw, dt_bias_row, alog_row, bsz, seq):
    m = xs.shape[0]
    nc = seq // SSM_CHUNK
    L = SSM_CHUNK
    n_bc = SSM_GROUPS * SSM_STATE
    fwd = lambda b, c: b * nc + c
    bwd = lambda b, c: b * nc + nc - 1 - c

    def operands(ch):
        return [pl.BlockSpec((L, SSM_D_INNER), lambda b, c: (ch(b, c), 0)),
                pl.BlockSpec((n_bc, L), lambda b, c: (0, ch(b, c))),
                pl.BlockSpec((L, n_bc), lambda b, c: (ch(b, c), 0)),
                pl.BlockSpec((L, LANES), lambda b, c: (ch(b, c), 0))]

    scratch = [pltpu.VMEM((SSM_GROUPS, SSM_STATE, SSM_GROUP_W), F32),
               pltpu.VMEM((SSM_GROUPS, SSM_HEADS_PER_GROUP * L, SSM_GROUP_W), BF16),
               pltpu.VMEM((LANES, L), F32)]
    y_shape = jax.ShapeDtypeStruct((m, SSM_D_INNER), BF16)
    return pl.pallas_call(
        _ssd_kernel,
        grid=(bsz, nc),
        in_specs=operands(fwd) + operands(bwd) + [_resident((1, LANES)), _resident((1, LANES))],
        out_specs=[pl.BlockSpec((L, SSM_D_INNER), lambda b, c: (fwd(b, c), 0)),
                   pl.BlockSpec((L, SSM_D_INNER), lambda b, c: (bwd(b, c), 0))],
        out_shape=[y_shape, y_shape],
        scratch_shapes=scratch + scratch,
        compiler_params=_params("parallel", "arbitrary"),
        name="ssd_scan",
    )(xs, b_t, cm, dt_raw, xs, b_t, cm, dt_raw, dt_bias_row, alog_row)


def kernel(x, attn_w_in, attn_sink, attn_rpb, attn_w_out, ssm_w_in, ssm_conv_w, ssm_conv_b, ssm_dt_bias,
           ssm_A_log, ssm_D, ssm_norm_w, ssm_w_out, mlp_w1, mlp_w2, ln1_g, ln1_b, ln2_g, ln2_b):
    bsz, seq, _ = x.shape
    x2d = x.reshape(bsz * seq, D_MODEL)

    qa, ka, va, qb, kb, vb = _attn_in_proj(x2d, attn_w_in[0].astype(BF16), seq)
    oa = _win_attn(qa, ka, va, attn_sink[0], bsz, seq)
    ob = _na_attn(qb, kb, vb, attn_rpb[0], bsz, seq)
    x2d = _attn_tail(oa, ob, x2d, attn_w_out[0].astype(BF16), ln1_g[0], ln1_b[0],
                     mlp_w1[0].astype(BF16), mlp_w2[0].astype(BF16), ln2_g[0], ln2_b[0])

    n_dt = 2 * SSM_HEADS
    w_pad = jnp.pad(ssm_w_in[0].astype(BF16), ((0, 0), (0, LANES - n_dt)))
    z, xs, b_t, cm, dt_raw = _ssm_in_proj(x2d, w_pad, ssm_conv_w[0], ssm_conv_b[0], seq)
    lane_pad = lambda a: jnp.pad(a.reshape(1, n_dt).astype(F32), ((0, 0), (0, LANES - n_dt)))
    y_f, y_b = _ssd_scan(xs, b_t, cm, dt_raw, lane_pad(ssm_dt_bias[0]), lane_pad(ssm_A_log[0]), bsz, seq)
    x2d = _ssm_tail(y_f, y_b, xs, z, ssm_D[0], ssm_norm_w[0], x2d, ssm_w_out[0].astype(BF16),
                    ln1_g[1], ln1_b[1], mlp_w1[1].astype(BF16), mlp_w2[1].astype(BF16),
                    ln2_g[1], ln2_b[1])
    return x2d.reshape(bsz, seq, D_MODEL)
```

```python
import functools

import numpy as np
import jax
import jax.numpy as jnp
from jax import lax
from jax.experimental import pallas as pl
from jax.experimental.pallas import tpu as pltpu

F32 = jnp.float32
BF16 = jnp.bfloat16

D_MODEL = 1024
HEAD_DIM = 64
A_Q_HEADS = 8
A_KV_HEADS = 2
A_BLOCK = 128
WIN_QBLOCKS = 4
ROPE_THETA = 10000.0
B_HEADS = 8
GRID_W = 64
NA_KH = 8
NA_KW = 16
NA_QROWS = 4
NA_KROWS = 12
A_Q_DIM = A_Q_HEADS * HEAD_DIM
A_KV_DIM = A_KV_HEADS * HEAD_DIM
B_DIM = B_HEADS * HEAD_DIM
SSM_D_INNER = 2 * D_MODEL
SSM_HEADS = 32
SSM_GROUPS = 8
SSM_STATE = 128
SSM_CONV = 5
SSM_CHUNK = 128
SSM_GROUP_W = SSM_D_INNER // SSM_GROUPS
SSM_HEADS_PER_GROUP = SSM_HEADS // SSM_GROUPS
SSM_CONV_DIM = SSM_D_INNER + 2 * SSM_GROUPS * SSM_STATE
D_FF = 4 * D_MODEL
DEPTH = 2
DN_ALPHA = (2 * DEPTH) ** 0.25
LN_EPS = 1e-5
RMS_EPS = 1e-5
NEG_INF = -1e30
LOG2E = 1.4426950408889634
LANES = 128
HALO = 8
CONV_PHASES = 4

VMEM_LIMIT = 56 * 1024 * 1024


def _params(*sem):
    return pltpu.CompilerParams(dimension_semantics=sem, vmem_limit_bytes=VMEM_LIMIT)


def _resident(shape):
    nd = len(shape)
    return pl.BlockSpec(shape, lambda *_: (0,) * nd, pipeline_mode=pl.Buffered(1))


def _layer_norm(v, g, b):
    mu = jnp.mean(v, axis=-1, keepdims=True)
    d = v - mu
    var = jnp.mean(d * d, axis=-1, keepdims=True)
    return d * lax.rsqrt(var + LN_EPS) * g + b


def _silu(v):
    return v / (1.0 + jnp.exp2(v * (-LOG2E)))


def _attn_in_kernel(x_ref, w_ref, cos_ref, sin_ref, qa_ref, ka_ref, va_ref, qb_ref, kb_ref, vb_ref):
    tm = x_ref.shape[0]
    xb = x_ref[...].astype(BF16)
    cos = cos_ref[...]
    sin = sin_ref[...]
    lane = lax.broadcasted_iota(jnp.int32, (tm, LANES), 1)
    first_half = (lane % HEAD_DIM) < (HEAD_DIM // 2)
    scale = HEAD_DIM ** -0.5

    def rope(h):
        partner = jnp.where(first_half, pltpu.roll(h, LANES - HEAD_DIM // 2, 1),
                            pltpu.roll(h, HEAD_DIM // 2, 1))
        return h * cos + partner * sin

    n_rope = A_Q_DIM + A_KV_DIM
    h = jnp.dot(xb, w_ref[:, :n_rope], preferred_element_type=F32)
    for j in range(A_Q_DIM // LANES):
        qa_ref[:, j * LANES:(j + 1) * LANES] = (rope(h[:, j * LANES:(j + 1) * LANES]) * scale).astype(BF16)
    for j in range(A_KV_DIM // LANES):
        c0 = A_Q_DIM + j * LANES
        ka_ref[j * LANES:(j + 1) * LANES, :] = rope(h[:, c0:c0 + LANES]).T.astype(BF16)
    c0 = n_rope
    va_ref[...] = jnp.dot(xb, w_ref[:, c0:c0 + A_KV_DIM], preferred_element_type=F32).astype(BF16)
    c0 += A_KV_DIM
    qb_ref[...] = (jnp.dot(xb, w_ref[:, c0:c0 + B_DIM], preferred_element_type=F32) * scale).astype(BF16)
    c0 += B_DIM
    kb_ref[...] = jnp.dot(xb, w_ref[:, c0:c0 + B_DIM], preferred_element_type=F32).astype(BF16).T
    c0 += B_DIM
    vb_ref[...] = jnp.dot(xb, w_ref[:, c0:c0 + B_DIM], preferred_element_type=F32).astype(BF16)


def _attn_in_proj(x2d, w_bf16, seq):
    m = x2d.shape[0]
    tm = 1024
    half = HEAD_DIM // 2
    inv = ROPE_THETA ** (-jnp.arange(half, dtype=F32) / half)
    ang = jnp.arange(seq, dtype=F32)[:, None] * inv[None, :]
    reps = LANES // half
    cos_t = jnp.tile(jnp.cos(ang), (1, reps))
    sign = jnp.tile(jnp.concatenate([-jnp.ones((half,), F32), jnp.ones((half,), F32)]), LANES // HEAD_DIM)
    sin_t = jnp.tile(jnp.sin(ang), (1, reps)) * sign[None, :]
    tpb = seq // tm
    row = lambda i: (i, 0)
    pos = lambda i: (i % tpb, 0)
    col = lambda i: (0, i)
    tok = lambda n: (pl.BlockSpec((tm, n), row), jax.ShapeDtypeStruct((m, n), BF16))
    chan = lambda n: (pl.BlockSpec((n, tm), col), jax.ShapeDtypeStruct((n, m), BF16))
    outs = [tok(A_Q_DIM), chan(A_KV_DIM), tok(A_KV_DIM), tok(B_DIM), chan(B_DIM), tok(B_DIM)]
    return pl.pallas_call(
        _attn_in_kernel,
        grid=(m // tm,),
        in_specs=[pl.BlockSpec((tm, D_MODEL), row), _resident(w_bf16.shape),
                  pl.BlockSpec((tm, LANES), pos), pl.BlockSpec((tm, LANES), pos)],
        out_specs=[o[0] for o in outs],
        out_shape=[o[1] for o in outs],
        compiler_params=_params("parallel"),
        name="attn_in_proj",
    )(x2d, w_bf16, cos_t, sin_t)


def _win_attn_kernel(sink_ref, q_ref, kp_ref, kc_ref, kn_ref, vp_ref, vc_ref, vn_ref, o_ref):
    i = pl.program_id(1)
    ni = pl.num_programs(1)
    blk = A_BLOCK
    rep = A_Q_HEADS // A_KV_HEADS
    qi = lax.broadcasted_iota(jnp.int32, (rep * blk, 3 * blk), 0) % blk
    ki = lax.broadcasted_iota(jnp.int32, (rep * blk, 3 * blk), 1)
    band = jnp.abs(qi + blk - ki) <= blk
    kt_all = jnp.concatenate([kp_ref[...], kc_ref[...], kn_ref[...]], axis=1)
    v_all = jnp.concatenate([vp_ref[...], vc_ref[...], vn_ref[...]], axis=0)
    for sb in range(WIN_QBLOCKS):
        valid = band
        if sb == 0:
            valid = valid & ((ki >= blk) | (i > 0))
        if sb == WIN_QBLOCKS - 1:
            valid = valid & ((ki < 2 * blk) | (i < ni - 1))
        rows = slice(sb * blk, (sb + 1) * blk)
        for g in range(A_KV_HEADS):
            kg_t = kt_all[g * HEAD_DIM:(g + 1) * HEAD_DIM, sb * blk:(sb + 3) * blk]
            vg = v_all[sb * blk:(sb + 3) * blk, g * HEAD_DIM:(g + 1) * HEAD_DIM]
            heads = [g * rep + r for r in range(rep)]
            q4 = jnp.concatenate([q_ref[rows, h * HEAD_DIM:(h + 1) * HEAD_DIM] for h in heads], axis=0)
            s = jnp.dot(q4, kg_t, preferred_element_type=F32)
            s = jnp.where(valid, s, NEG_INF)
            ps, dens = [], []
            for r, h in enumerate(heads):
                sr = s[r * blk:(r + 1) * blk]
                sink = sink_ref[h]
                mx = jnp.maximum(jnp.max(sr, axis=-1, keepdims=True), sink)
                p = jnp.exp(sr - mx)
                dens.append(jnp.sum(p, axis=-1, keepdims=True) + jnp.exp(sink - mx))
                ps.append(p.astype(BF16))
            o4 = jnp.dot(jnp.concatenate(ps, axis=0), vg, preferred_element_type=F32)
            for r, h in enumerate(heads):
                o = o4[r * blk:(r + 1) * blk] / dens[r]
                o_ref[rows, h * HEAD_DIM:(h + 1) * HEAD_DIM] = o.astype(BF16)


def _win_attn(qa, ka, va, sink, bsz, seq):
    nb = seq // A_BLOCK
    tq = WIN_QBLOCKS * A_BLOCK
    ni = seq // tq
    cur = lambda b, i: (b * ni + i, 0)
    prv = lambda b, i: (b * nb + jnp.maximum(i * WIN_QBLOCKS - 1, 0), 0)
    nxt = lambda b, i: (b * nb + jnp.minimum((i + 1) * WIN_QBLOCKS, nb - 1), 0)
    edge = lambda im: pl.BlockSpec((A_BLOCK, A_KV_DIM), im)
    body = pl.BlockSpec((tq, A_KV_DIM), cur)
    flip = lambda im: (lambda b, i: im(b, i)[::-1])
    edge_t = lambda im: pl.BlockSpec((A_KV_DIM, A_BLOCK), flip(im))
    body_t = pl.BlockSpec((A_KV_DIM, tq), flip(cur))
    return pl.pallas_call(
        _win_attn_kernel,
        grid=(bsz, ni),
        in_specs=[pl.BlockSpec(memory_space=pltpu.SMEM),
                  pl.BlockSpec((tq, A_Q_DIM), cur),
                  edge_t(prv), body_t, edge_t(nxt), edge(prv), body, edge(nxt)],
        out_specs=pl.BlockSpec((tq, A_Q_DIM), cur),
        out_shape=jax.ShapeDtypeStruct(qa.shape, BF16),
        compiler_params=_params("parallel", "parallel"),
        name="window_attn",
    )(sink.astype(F32), qa, ka, ka, ka, va, va, va)


def _na_row_tables(rows):
    nj = rows // NA_QROWS
    assert rows >= NA_KROWS + NA_QROWS and rows % NA_QROWS == 0

    def tile(j):
        ws = min(max(NA_QROWS * j - NA_KH // 2, 0), rows - NA_KROWS)
        r = NA_QROWS * j + np.arange(NA_QROWS)[:, None]
        rk = ws + np.arange(NA_KROWS)[None, :]
        rs = np.clip(r - NA_KH // 2, 0, rows - NA_KH)
        return (rk >= rs) & (rk < rs + NA_KH), rk - r + NA_KH - 1

    cases = [tile(0), tile(1), tile(nj - 1)]
    for j in range(1, nj - 1):
        ok, dr = tile(j)
        assert np.array_equal(ok, cases[1][0]) and np.array_equal(np.where(ok, dr, 0), np.where(ok, cases[1][1], 0))
    return np.stack([c[0] for c in cases]), np.stack([c[1] for c in cases])


def _na_bias(rpb, rows):
    ok_r, dr = _na_row_tables(rows)
    w = GRID_W
    pad = w - NA_KW
    padded = jnp.pad(rpb.astype(F32), ((0, 0), (0, 0), (pad, pad)))
    toep = jnp.stack([padded[:, :, w - 1 - cq:2 * w - 1 - cq] for cq in range(w)], axis=2)
    cq = np.arange(w)[:, None]
    ck = np.arange(w)[None, :]
    cs = np.clip(cq - NA_KW // 2, 0, w - NA_KW)
    toep = jnp.where((ck >= cs) & (ck < cs + NA_KW), toep, NEG_INF)
    n_dr = toep.shape[1]
    wide = toep.transpose(0, 2, 1, 3).reshape(B_HEADS, w, n_dr * w)
    outside = jnp.full((B_HEADS, w, NA_KROWS * w), NEG_INF, F32)
    tiles = []
    for case in range(ok_r.shape[0]):
        q_rows = []
        for rq in range(NA_QROWS):
            pieces, rk = [], 0
            while rk < NA_KROWS:
                end = rk + 1
                inside = bool(ok_r[case, rq, rk])
                while end < NA_KROWS and bool(ok_r[case, rq, end]) == inside and (
                        not inside or dr[case, rq, end] == dr[case, rq, end - 1] + 1):
                    end += 1
                d0 = int(dr[case, rq, rk])
                pieces.append(wide[:, :, d0 * w:(d0 + end - rk) * w] if inside
                              else outside[:, :, :(end - rk) * w])
                rk = end
            q_rows.append(jnp.concatenate(pieces, axis=-1))
        tiles.append(jnp.concatenate(q_rows, axis=1))
    return jnp.stack(tiles)


def _na_kernel(q_ref, k0_ref, k1_ref, k2_ref, v0_ref, v1_ref, v2_ref, bias_ref, o_ref):
    kt_all = jnp.concatenate([k0_ref[...], k1_ref[...], k2_ref[...]], axis=1)
    v_all = jnp.concatenate([v0_ref[...], v1_ref[...], v2_ref[...]], axis=0)
    for h in range(B_HEADS):
        sl = slice(h * HEAD_DIM, (h + 1) * HEAD_DIM)
        s = jnp.dot(q_ref[:, sl], kt_all[sl, :], preferred_element_type=F32)
        s = s + bias_ref[0, h]
        mx = jnp.max(s, axis=-1, keepdims=True)
        p = jnp.exp(s - mx)
        den = jnp.sum(p, axis=-1, keepdims=True)
        o = jnp.dot(p.astype(BF16), v_all[:, sl], preferred_element_type=F32) / den
        o_ref[:, sl] = o.astype(BF16)


def _na_attn(qb, kb, vb, rpb, bsz, seq):
    rows = seq // GRID_W
    nj = rows // NA_QROWS
    bias = _na_bias(rpb, rows)
    tq = NA_QROWS * GRID_W
    kblocks = NA_KROWS // NA_QROWS
    per_b = seq // tq
    qmap = lambda b, j: (b * per_b + j, 0)

    def kmap(i):
        return lambda b, j: (b * per_b + jnp.clip(j - 1, 0, nj - kblocks) + i, 0)

    case = lambda b, j: (jnp.where(j == 0, 0, jnp.where(j == nj - 1, 2, 1)), 0, 0, 0)
    kv = [pl.BlockSpec((tq, B_DIM), kmap(i)) for i in range(kblocks)]
    kv_t = [pl.BlockSpec((B_DIM, tq), (lambda f: (lambda b, j: f(b, j)[::-1]))(kmap(i))) for i in range(kblocks)]
    return pl.pallas_call(
        _na_kernel,
        grid=(bsz, nj),
        in_specs=[pl.BlockSpec((tq, B_DIM), qmap)] + kv_t + kv
                 + [pl.BlockSpec((1, B_HEADS, tq, NA_KROWS * GRID_W), case)],
        out_specs=pl.BlockSpec((tq, B_DIM), qmap),
        out_shape=jax.ShapeDtypeStruct(qb.shape, BF16),
        compiler_params=_params("parallel", "arbitrary"),
        name="neighborhood_attn",
    )(qb, kb, kb, kb, vb, vb, vb, bias)


def _post_ln_mlp(x, mix, g1, b1, w1_ref, w2_ref, g2, b2):
    x1 = _layer_norm(DN_ALPHA * x + mix, g1, b1)
    xb = x1.astype(BF16)
    acc = jnp.zeros_like(x1)
    fc = 1024
    for c in range(D_FF // fc):
        h = jnp.dot(xb, w1_ref[:, c * fc:(c + 1) * fc], preferred_element_type=F32)
        h = jnp.maximum(h, 0.0)
        acc = acc + jnp.dot((h * h).astype(BF16), w2_ref[c * fc:(c + 1) * fc, :],
                            preferred_element_type=F32)
    return _layer_norm(DN_ALPHA * x1 + acc, g2, b2)


def _attn_tail_kernel(oa_ref, ob_ref, x_ref, wo_ref, g1_ref, b1_ref, w1_ref, w2_ref, g2_ref, b2_ref,
                      out_ref):
    mix = (jnp.dot(oa_ref[...], wo_ref[:A_Q_DIM, :], preferred_element_type=F32)
           + jnp.dot(ob_ref[...], wo_ref[A_Q_DIM:, :], preferred_element_type=F32))
    out_ref[...] = _post_ln_mlp(x_ref[...], mix, g1_ref[...], b1_ref[...], w1_ref, w2_ref,
                                g2_ref[...], b2_ref[...])


def _attn_tail(oa, ob, x2d, wo, g1, b1, w1, w2, g2, b2):
    m = x2d.shape[0]
    tm = 1024
    row = lambda i: (i, 0)
    vec = lambda a: a.reshape(1, -1).astype(F32)
    return pl.pallas_call(
        _attn_tail_kernel,
        grid=(m // tm,),
        in_specs=[pl.BlockSpec((tm, A_Q_DIM), row), pl.BlockSpec((tm, B_DIM), row),
                  pl.BlockSpec((tm, D_MODEL), row), _resident(wo.shape),
                  _resident((1, D_MODEL)), _resident((1, D_MODEL)),
                  _resident(w1.shape), _resident(w2.shape),
                  _resident((1, D_MODEL)), _resident((1, D_MODEL))],
        out_specs=pl.BlockSpec((tm, D_MODEL), row),
        out_shape=jax.ShapeDtypeStruct((m, D_MODEL), F32),
        compiler_params=_params("parallel"),
        name="attn_out_mlp",
    )(oa, ob, x2d, wo, vec(g1), vec(b1), w1, w2, vec(g2), vec(b2))


def _ssm_tail_kernel(yf_ref, yb_ref, xs_ref, z_ref, dskip_ref, nw_ref, x_ref, wo_ref, g1_ref, b1_ref,
                     w1_ref, w2_ref, g2_ref, b2_ref, out_ref):
    y = yf_ref[...].astype(F32) + yb_ref[...].astype(F32) + xs_ref[...].astype(F32) * dskip_ref[...]
    y = y * z_ref[...].astype(F32)
    parts = []
    for g in range(SSM_GROUPS):
        yg = y[:, g * SSM_GROUP_W:(g + 1) * SSM_GROUP_W]
        ms = jnp.mean(yg * yg, axis=-1, keepdims=True)
        parts.append(yg * lax.rsqrt(ms + RMS_EPS))
    yn = (jnp.concatenate(parts, axis=-1) * nw_ref[...]).astype(BF16)
    mix = jnp.dot(yn, wo_ref[...], preferred_element_type=F32)
    out_ref[...] = _post_ln_mlp(x_ref[...], mix, g1_ref[...], b1_ref[...], w1_ref, w2_ref,
                                g2_ref[...], b2_ref[...])


def _ssm_tail(yf, yb, xbc, z, d_skip, norm_w, x2d, wo, g1, b1, w1, w2, g2, b2):
    m = x2d.shape[0]
    tm = 512
    row = lambda i: (i, 0)
    vec = lambda a: a.reshape(1, -1).astype(F32)
    wide = pl.BlockSpec((tm, SSM_D_INNER), row)
    dskip_e = jnp.repeat(d_skip.astype(F32), SSM_D_INNER // SSM_HEADS).reshape(1, -1)
    return pl.pallas_call(
        _ssm_tail_kernel,
        grid=(m // tm,),
        in_specs=[wide, wide, wide, wide, _resident((1, SSM_D_INNER)), _resident((1, SSM_D_INNER)),
                  pl.BlockSpec((tm, D_MODEL), row), _resident(wo.shape),
                  _resident((1, D_MODEL)), _resident((1, D_MODEL)),
                  _resident(w1.shape), _resident(w2.shape),
                  _resident((1, D_MODEL)), _resident((1, D_MODEL))],
        out_specs=pl.BlockSpec((tm, D_MODEL), row),
        out_shape=jax.ShapeDtypeStruct((m, D_MODEL), F32),
        compiler_params=_params("parallel"),
        name="ssm_out_mlp",
    )(yf, yb, xbc, z, dskip_e, vec(norm_w), x2d, wo, vec(g1), vec(b1), w1, w2, vec(g2), vec(b2))


def _ssm_in_kernel(tiles_per_seq, prev_ref, x_ref, next_ref, w_ref, cw_ref, cb_ref,
                   z_ref, xs_ref, bt_ref, c_ref, dt_ref, raw_ref, act_ref):
    tm = x_ref.shape[0]
    pos = pl.program_id(0) % tiles_per_seq
    x = x_ref[...]
    xb = x.astype(BF16)
    nc = 1024
    for c in range(SSM_D_INNER // nc):
        z_ref[:, c * nc:(c + 1) * nc] = _silu(jnp.dot(
            xb, w_ref[:, c * nc:(c + 1) * nc], preferred_element_type=F32)).astype(BF16)
    c0 = SSM_D_INNER + SSM_CONV_DIM
    dt_ref[...] = jnp.dot(xb, w_ref[:, c0:c0 + LANES], preferred_element_type=F32)
    prev = jnp.where(pos > 0, prev_ref[...], 0.0)
    nxt = jnp.where(pos < tiles_per_seq - 1, next_ref[...], 0.0)
    xe = jnp.concatenate([prev, x, nxt], axis=0).astype(BF16)
    pad = SSM_CONV // 2
    groups = tm // CONV_PHASES
    for c in range(SSM_CONV_DIM // nc):
        cols = slice(c * nc, (c + 1) * nc)
        raw = jnp.dot(xe, w_ref[:, SSM_D_INNER + c * nc:SSM_D_INNER + (c + 1) * nc],
                      preferred_element_type=F32)
        for j in range(nc // LANES):
            lanes = slice(c * nc + j * LANES, c * nc + (j + 1) * LANES)
            raw_ref[j] = raw[:, j * LANES:(j + 1) * LANES]
            slabs = {}
            for ph in range(CONV_PHASES):
                acc = cb_ref[:, lanes]
                for k in range(SSM_CONV):
                    m = HALO + ph + k - pad
                    if m not in slabs:
                        slabs[m] = raw_ref[j, pl.ds(m, groups, stride=CONV_PHASES), :]
                    acc = acc + slabs[m] * cw_ref[k:k + 1, lanes]
                act_ref[j, pl.ds(ph, groups, stride=CONV_PHASES), :] = _silu(acc)
            act = act_ref[j].astype(BF16)
            lo = c * nc + j * LANES
            if lo < SSM_D_INNER:
                xs_ref[:, lo:lo + LANES] = act
            elif lo < SSM_D_INNER + SSM_GROUPS * SSM_STATE:
                lo -= SSM_D_INNER
                bt_ref[lo:lo + LANES, :] = act.T
            else:
                lo -= SSM_D_INNER + SSM_GROUPS * SSM_STATE
                c_ref[:, lo:lo + LANES] = act


def _ssm_in_proj(x2d, w_pad, conv_w, conv_b, seq):
    m = x2d.shape[0]
    tm = 512
    tpb = seq // tm
    hb = tm // HALO
    last = m // HALO - 1
    n_bc = SSM_GROUPS * SSM_STATE
    row = lambda i: (i, 0)
    return pl.pallas_call(
        functools.partial(_ssm_in_kernel, tpb),
        grid=(m // tm,),
        in_specs=[pl.BlockSpec((HALO, D_MODEL), lambda i: (jnp.maximum(i * hb - 1, 0), 0)),
                  pl.BlockSpec((tm, D_MODEL), row),
                  pl.BlockSpec((HALO, D_MODEL), lambda i: (jnp.minimum((i + 1) * hb, last), 0)),
                  _resident(w_pad.shape), _resident((SSM_CONV, SSM_CONV_DIM)), _resident((1, SSM_CONV_DIM))],
        out_specs=[pl.BlockSpec((tm, SSM_D_INNER), row), pl.BlockSpec((tm, SSM_D_INNER), row),
                   pl.BlockSpec((n_bc, tm), lambda i: (0, i)), pl.BlockSpec((tm, n_bc), row),
                   pl.BlockSpec((tm, LANES), row)],
        out_shape=[jax.ShapeDtypeStruct((m, SSM_D_INNER), BF16),
                   jax.ShapeDtypeStruct((m, SSM_D_INNER), BF16),
                   jax.ShapeDtypeStruct((n_bc, m), BF16),
                   jax.ShapeDtypeStruct((m, n_bc), BF16),
                   jax.ShapeDtypeStruct((m, LANES), F32)],
        scratch_shapes=[pltpu.VMEM((1024 // LANES, tm + 2 * HALO, LANES), F32),
                        pltpu.VMEM((1024 // LANES, tm, LANES), F32)],
        compiler_params=_params("parallel"),
        name="ssm_in_conv",
    )(x2d, x2d, x2d, w_pad, conv_w.astype(F32), conv_b.reshape(1, -1).astype(F32))


def _ssd_chunk(reverse, xs_ref, bt_ref, c_ref, dtraw_ref, dtb_ref, alog_ref, y_ref,
               state_ref, blk_ref, rt_ref):
    L = SSM_CHUNK
    hpg = SSM_HEADS_PER_GROUP
    hw = SSM_GROUP_W // hpg
    off = SSM_HEADS if reverse else 0

    v = dtraw_ref[...] + dtb_ref[...]
    dt = jnp.maximum(v, 0.0) + jnp.log1p(jnp.exp(-jnp.abs(v)))
    a2 = dt * (-jnp.exp(alog_ref[...])) * LOG2E
    li = lax.broadcasted_iota(jnp.int32, (L, L), 0)
    si = lax.broadcasted_iota(jnp.int32, (L, L), 1)
    ac2 = a2
    s = 1
    while s < L:
        if reverse:
            ac2 = ac2 + jnp.where(li < L - s, pltpu.roll(ac2, L - s, 0), 0.0)
        else:
            ac2 = ac2 + jnp.where(li >= s, pltpu.roll(ac2, s, 0), 0.0)
        s *= 2
    tot2 = ac2[0:1, :] if reverse else ac2[L - 1:L, :]
    ldt = jnp.log2(dt)
    rt_ref[...] = (ac2 - ldt).T
    from_start = jnp.exp2(ac2)
    to_end = jnp.exp2(tot2 - ac2 + ldt)
    chunk_decay = jnp.exp2(tot2)
    tri = (si >= li) if reverse else (li >= si)

    lane = lax.broadcasted_iota(jnp.int32, (L, LANES), 1)

    def expand(src, g):
        if src.shape[0] != L:
            return jnp.concatenate(
                [jnp.broadcast_to(src[:, off + g * hpg + r:off + g * hpg + r + 1], (src.shape[0], hw))
                 for r in range(hpg)], axis=1)
        return jnp.concatenate(
            [jnp.take_along_axis(src, off + g * hpg + 2 * j + lane // hw, axis=1, mode="promise_in_bounds")
             for j in range(hpg // 2)], axis=1)

    for g in range(SSM_GROUPS):
        bg_t = bt_ref[g * SSM_STATE:(g + 1) * SSM_STATE, :]
        cg = c_ref[:, g * SSM_STATE:(g + 1) * SSM_STATE]
        cb = jnp.dot(cg, bg_t, preferred_element_type=F32)
        ms = []
        for r in range(hpg):
            h = off + g * hpg + r
            col = jnp.broadcast_to(ac2[:, h:h + 1], (L, L))
            lm = jnp.exp2(jnp.where(tri, col - rt_ref[h:h + 1, :], NEG_INF))
            ms.append((cb * lm).astype(BF16))
            c0 = g * SSM_GROUP_W + r * hw
            blk_ref[g, r * L:(r + 1) * L, r * hw:(r + 1) * hw] = xs_ref[:, c0:c0 + hw]
        prev = state_ref[g]
        y = jnp.dot(jnp.concatenate(ms, axis=1), blk_ref[g], preferred_element_type=F32)
        y = y + jnp.dot(cg, prev.astype(BF16), preferred_element_type=F32) * expand(from_start, g)
        y_ref[:, g * SSM_GROUP_W:(g + 1) * SSM_GROUP_W] = y.astype(y_ref.dtype)
        xg = xs_ref[:, g * SSM_GROUP_W:(g + 1) * SSM_GROUP_W].astype(F32)
        xd = (xg * expand(to_end, g)).astype(BF16)
        contrib = jnp.dot(bg_t, xd, preferred_element_type=F32)
        state_ref[g] = prev * expand(chunk_decay, g) + contrib


def _ssd_kernel(xf_ref, bf_ref, cf_ref, dtf_ref, xb_ref, bb_ref, cb_ref, dtb_ref, bias_ref, alog_ref,
                yf_ref, yb_ref, sf_ref, kf_ref, rf_ref, sb_ref, kb_ref, rb_ref):
    @pl.when(pl.program_id(1) == 0)
    def _():
        for ref in (sf_ref, kf_ref, sb_ref, kb_ref):
            ref[...] = jnp.zeros_like(ref)

    _ssd_chunk(False, xf_ref, bf_ref, cf_ref, dtf_ref, bias_ref, alog_ref, yf_ref, sf_ref, kf_ref, rf_ref)
    _ssd_chunk(True, xb_ref, bb_ref, cb_ref, dtb_ref, bias_ref, alog_ref, yb_ref, sb_ref, kb_ref, rb_ref)


def _ssd_scan(xs, b_t, cm, dt_raw, dt_bias_row, alog_row, bsz, seq):
    m = xs.shape[0]
    nc = seq // SSM_CHUNK
    L = SSM_CHUNK
    n_bc = SSM_GROUPS * SSM_STATE
    fwd = lambda b, c: b * nc + c
    bwd = lambda b, c: b * nc + nc - 1 - c

    def operands(ch):
        return [pl.BlockSpec((L, SSM_D_INNER), lambda b, c: (ch(b, c), 0)),
                pl.BlockSpec((n_bc, L), lambda b, c: (0, ch(b, c))),
                pl.BlockSpec((L, n_bc), lambda b, c: (ch(b, c), 0)),
                pl.BlockSpec((L, LANES), lambda b, c: (ch(b, c), 0))]

    scratch = [pltpu.VMEM((SSM_GROUPS, SSM_STATE, SSM_GROUP_W), F32),
               pltpu.VMEM((SSM_GROUPS, SSM_HEADS_PER_GROUP * L, SSM_GROUP_W), BF16),
               pltpu.VMEM((LANES, L), F32)]
    y_shape = jax.ShapeDtypeStruct((m, SSM_D_INNER), BF16)
    return pl.pallas_call(
        _ssd_kernel,
        grid=(bsz, nc),
        in_specs=operands(fwd) + operands(bwd) + [_resident((1, LANES)), _resident((1, LANES))],
        out_specs=[pl.BlockSpec((L, SSM_D_INNER), lambda b, c: (fwd(b, c), 0)),
                   pl.BlockSpec((L, SSM_D_INNER), lambda b, c: (bwd(b, c), 0))],
        out_shape=[y_shape, y_shape],
        scratch_shapes=scratch + scratch,
        compiler_params=_params("parallel", "arbitrary"),
        name="ssd_scan",
    )(xs, b_t, cm, dt_raw, xs, b_t, cm, dt_raw, dt_bias_row, alog_row)


def kernel(x, attn_w_in, attn_sink, attn_rpb, attn_w_out, ssm_w_in, ssm_conv_w, ssm_conv_b, ssm_dt_bias,
           ssm_A_log, ssm_D, ssm_norm_w, ssm_w_out, mlp_w1, mlp_w2, ln1_g, ln1_b, ln2_g, ln2_b):
    bsz, seq, _ = x.shape
    x2d = x.reshape(bsz * seq, D_MODEL)

    qa, ka, va, qb, kb, vb = _attn_in_proj(x2d, attn_w_in[0].astype(BF16), seq)
    oa = _win_attn(qa, ka, va, attn_sink[0], bsz, seq)
    ob = _na_attn(qb, kb, vb, attn_rpb[0], bsz, seq)
    x2d = _attn_tail(oa, ob, x2d, attn_w_out[0].astype(BF16), ln1_g[0], ln1_b[0],
                     mlp_w1[0].astype(BF16), mlp_w2[0].astype(BF16), ln2_g[0], ln2_b[0])

    n_dt = 2 * SSM_HEADS
    w_pad = jnp.pad(ssm_w_in[0].astype(BF16), ((0, 0), (0, LANES - n_dt)))
    z, xs, b_t, cm, dt_raw = _ssm_in_proj(x2d, w_pad, ssm_conv_w[0], ssm_conv_b[0], seq)
    lane_pad = lambda a: jnp.pad(a.reshape(1, n_dt).astype(F32), ((0, 0), (0, LANES - n_dt)))
    y_f, y_b = _ssd_scan(xs, b_t, cm, dt_raw, lane_pad(ssm_dt_bias[0]), lane_pad(ssm_A_log[0]), bsz, seq)
    x2d = _ssm_tail(y_f, y_b, xs, z, ssm_D[0], ssm_norm_w[0], x2d, ssm_w_out[0].astype(BF16),
                    ln1_g[1], ln1_b[1], mlp_w1[1].astype(BF16), mlp_w2[1].astype(BF16),
                    ln2_g[1], ln2_b[1])
    return x2d.reshape(bsz, seq, D_MODEL)
```

```python
import functools

import numpy as np
import jax
import jax.numpy as jnp
from jax import lax
from jax.experimental import pallas as pl
from jax.experimental.pallas import tpu as pltpu

F32 = jnp.float32
BF16 = jnp.bfloat16

D_MODEL = 1024
HEAD_DIM = 64
A_Q_HEADS = 8
A_KV_HEADS = 2
A_BLOCK = 128
WIN_QBLOCKS = 4
ROPE_THETA = 10000.0
B_HEADS = 8
GRID_W = 64
NA_KH = 8
NA_KW = 16
NA_QROWS = 4
NA_KROWS = 12
A_Q_DIM = A_Q_HEADS * HEAD_DIM
A_KV_DIM = A_KV_HEADS * HEAD_DIM
B_DIM = B_HEADS * HEAD_DIM
SSM_D_INNER = 2 * D_MODEL
SSM_HEADS = 32
SSM_GROUPS = 8
SSM_STATE = 128
SSM_CONV = 5
SSM_CHUNK = 128
SSD_CHUNKS_PER_STEP = 2
SSM_GROUP_W = SSM_D_INNER // SSM_GROUPS
SSM_HEADS_PER_GROUP = SSM_HEADS // SSM_GROUPS
SSM_CONV_DIM = SSM_D_INNER + 2 * SSM_GROUPS * SSM_STATE
D_FF = 4 * D_MODEL
DEPTH = 2
DN_ALPHA = (2 * DEPTH) ** 0.25
LN_EPS = 1e-5
RMS_EPS = 1e-5
NEG_INF = -1e30
LOG2E = 1.4426950408889634
LANES = 128
HALO = 8
CONV_PHASES = 4

VMEM_LIMIT = 56 * 1024 * 1024


def _params(*sem):
    return pltpu.CompilerParams(dimension_semantics=sem, vmem_limit_bytes=VMEM_LIMIT)


def _resident(shape):
    nd = len(shape)
    return pl.BlockSpec(shape, lambda *_: (0,) * nd, pipeline_mode=pl.Buffered(1))


def _layer_norm(v, g, b):
    mu = jnp.mean(v, axis=-1, keepdims=True)
    d = v - mu
    var = jnp.mean(d * d, axis=-1, keepdims=True)
    return d * lax.rsqrt(var + LN_EPS) * g + b


def _silu(v):
    return v / (1.0 + jnp.exp2(v * (-LOG2E)))


def _attn_in_kernel(x_ref, w_ref, cos_ref, sin_ref, qa_ref, ka_ref, va_ref, qb_ref, kb_ref, vb_ref):
    tm = x_ref.shape[0]
    xb = x_ref[...].astype(BF16)
    cos = cos_ref[...]
    sin = sin_ref[...]
    lane = lax.broadcasted_iota(jnp.int32, (tm, LANES), 1)
    first_half = (lane % HEAD_DIM) < (HEAD_DIM // 2)
    scale = HEAD_DIM ** -0.5

    def rope(h):
        partner = jnp.where(first_half, pltpu.roll(h, LANES - HEAD_DIM // 2, 1),
                            pltpu.roll(h, HEAD_DIM // 2, 1))
        return h * cos + partner * sin

    n_rope = A_Q_DIM + A_KV_DIM
    h = jnp.dot(xb, w_ref[:, :n_rope], preferred_element_type=F32)
    for j in range(A_Q_DIM // LANES):
        qa_ref[:, j * LANES:(j + 1) * LANES] = (rope(h[:, j * LANES:(j + 1) * LANES]) * scale).astype(BF16)
    for j in range(A_KV_DIM // LANES):
        c0 = A_Q_DIM + j * LANES
        ka_ref[j * LANES:(j + 1) * LANES, :] = rope(h[:, c0:c0 + LANES]).T.astype(BF16)
    c0 = n_rope
    va_ref[...] = jnp.dot(xb, w_ref[:, c0:c0 + A_KV_DIM], preferred_element_type=F32).astype(BF16)
    c0 += A_KV_DIM
    qb_ref[...] = (jnp.dot(xb, w_ref[:, c0:c0 + B_DIM], preferred_element_type=F32) * scale).astype(BF16)
    c0 += B_DIM
    kb_ref[...] = jnp.dot(xb, w_ref[:, c0:c0 + B_DIM], preferred_element_type=F32).astype(BF16).T
    c0 += B_DIM
    vb_ref[...] = jnp.dot(xb, w_ref[:, c0:c0 + B_DIM], preferred_element_type=F32).astype(BF16)


def _attn_in_proj(x2d, w_bf16, seq):
    m = x2d.shape[0]
    tm = 1024
    half = HEAD_DIM // 2
    inv = ROPE_THETA ** (-jnp.arange(half, dtype=F32) / half)
    ang = jnp.arange(seq, dtype=F32)[:, None] * inv[None, :]
    reps = LANES // half
    cos_t = jnp.tile(jnp.cos(ang), (1, reps))
    sign = jnp.tile(jnp.concatenate([-jnp.ones((half,), F32), jnp.ones((half,), F32)]), LANES // HEAD_DIM)
    sin_t = jnp.tile(jnp.sin(ang), (1, reps)) * sign[None, :]
    tpb = seq // tm
    row = lambda i: (i, 0)
    pos = lambda i: (i % tpb, 0)
    col = lambda i: (0, i)
    tok = lambda n: (pl.BlockSpec((tm, n), row), jax.ShapeDtypeStruct((m, n), BF16))
    chan = lambda n: (pl.BlockSpec((n, tm), col), jax.ShapeDtypeStruct((n, m), BF16))
    outs = [tok(A_Q_DIM), chan(A_KV_DIM), tok(A_KV_DIM), tok(B_DIM), chan(B_DIM), tok(B_DIM)]
    return pl.pallas_call(
        _attn_in_kernel,
        grid=(m // tm,),
        in_specs=[pl.BlockSpec((tm, D_MODEL), row), _resident(w_bf16.shape),
                  pl.BlockSpec((tm, LANES), pos), pl.BlockSpec((tm, LANES), pos)],
        out_specs=[o[0] for o in outs],
        out_shape=[o[1] for o in outs],
        compiler_params=_params("parallel"),
        name="attn_in_proj",
    )(x2d, w_bf16, cos_t, sin_t)


def _win_attn_kernel(sink_ref, q_ref, kp_ref, kc_ref, kn_ref, vp_ref, vc_ref, vn_ref, o_ref):
    i = pl.program_id(1)
    ni = pl.num_programs(1)
    blk = A_BLOCK
    rep = A_Q_HEADS // A_KV_HEADS
    qi = lax.broadcasted_iota(jnp.int32, (rep * blk, 3 * blk), 0) % blk
    ki = lax.broadcasted_iota(jnp.int32, (rep * blk, 3 * blk), 1)
    band = jnp.abs(qi + blk - ki) <= blk
    kt_all = jnp.concatenate([kp_ref[...], kc_ref[...], kn_ref[...]], axis=1)
    v_all = jnp.concatenate([vp_ref[...], vc_ref[...], vn_ref[...]], axis=0)
    for sb in range(WIN_QBLOCKS):
        valid = band
        if sb == 0:
            valid = valid & ((ki >= blk) | (i > 0))
        if sb == WIN_QBLOCKS - 1:
            valid = valid & ((ki < 2 * blk) | (i < ni - 1))
        rows = slice(sb * blk, (sb + 1) * blk)
        for g in range(A_KV_HEADS):
            kg_t = kt_all[g * HEAD_DIM:(g + 1) * HEAD_DIM, sb * blk:(sb + 3) * blk]
            vg = v_all[sb * blk:(sb + 3) * blk, g * HEAD_DIM:(g + 1) * HEAD_DIM]
            heads = [g * rep + r for r in range(rep)]
            q4 = jnp.concatenate([q_ref[rows, h * HEAD_DIM:(h + 1) * HEAD_DIM] for h in heads], axis=0)
            s = jnp.dot(q4, kg_t, preferred_element_type=F32)
            s = jnp.where(valid, s, NEG_INF)
            ps, dens = [], []
            for r, h in enumerate(heads):
                sr = s[r * blk:(r + 1) * blk]
                sink = sink_ref[h]
                mx = jnp.maximum(jnp.max(sr, axis=-1, keepdims=True), sink)
                p = jnp.exp(sr - mx)
                dens.append(jnp.sum(p, axis=-1, keepdims=True) + jnp.exp(sink - mx))
                ps.append(p.astype(BF16))
            o4 = jnp.dot(jnp.concatenate(ps, axis=0), vg, preferred_element_type=F32)
            for r, h in enumerate(heads):
                o = o4[r * blk:(r + 1) * blk] / dens[r]
                o_ref[rows, h * HEAD_DIM:(h + 1) * HEAD_DIM] = o.astype(BF16)


def _win_attn(qa, ka, va, sink, bsz, seq):
    nb = seq // A_BLOCK
    tq = WIN_QBLOCKS * A_BLOCK
    ni = seq // tq
    cur = lambda b, i: (b * ni + i, 0)
    prv = lambda b, i: (b * nb + jnp.maximum(i * WIN_QBLOCKS - 1, 0), 0)
    nxt = lambda b, i: (b * nb + jnp.minimum((i + 1) * WIN_QBLOCKS, nb - 1), 0)
    edge = lambda im: pl.BlockSpec((A_BLOCK, A_KV_DIM), im)
    body = pl.BlockSpec((tq, A_KV_DIM), cur)
    flip = lambda im: (lambda b, i: im(b, i)[::-1])
    edge_t = lambda im: pl.BlockSpec((A_KV_DIM, A_BLOCK), flip(im))
    body_t = pl.BlockSpec((A_KV_DIM, tq), flip(cur))
    return pl.pallas_call(
        _win_attn_kernel,
        grid=(bsz, ni),
        in_specs=[pl.BlockSpec(memory_space=pltpu.SMEM),
                  pl.BlockSpec((tq, A_Q_DIM), cur),
                  edge_t(prv), body_t, edge_t(nxt), edge(prv), body, edge(nxt)],
        out_specs=pl.BlockSpec((tq, A_Q_DIM), cur),
        out_shape=jax.ShapeDtypeStruct(qa.shape, BF16),
        compiler_params=_params("parallel", "parallel"),
        name="window_attn",
    )(sink.astype(F32), qa, ka, ka, ka, va, va, va)


def _na_row_tables(rows):
    nj = rows // NA_QROWS
    assert rows >= NA_KROWS + NA_QROWS and rows % NA_QROWS == 0

    def tile(j):
        ws = min(max(NA_QROWS * j - NA_KH // 2, 0), rows - NA_KROWS)
        r = NA_QROWS * j + np.arange(NA_QROWS)[:, None]
        rk = ws + np.arange(NA_KROWS)[None, :]
        rs = np.clip(r - NA_KH // 2, 0, rows - NA_KH)
        return (rk >= rs) & (rk < rs + NA_KH), rk - r + NA_KH - 1

    cases = [tile(0), tile(1), tile(nj - 1)]
    for j in range(1, nj - 1):
        ok, dr = tile(j)
        assert np.array_equal(ok, cases[1][0]) and np.array_equal(np.where(ok, dr, 0), np.where(ok, cases[1][1], 0))
    return np.stack([c[0] for c in cases]), np.stack([c[1] for c in cases])


def _na_bias(rpb, rows):
    ok_r, dr = _na_row_tables(rows)
    w = GRID_W
    pad = w - NA_KW
    padded = jnp.pad(rpb.astype(F32), ((0, 0), (0, 0), (pad, pad)))
    toep = jnp.stack([padded[:, :, w - 1 - cq:2 * w - 1 - cq] for cq in range(w)], axis=2)
    cq = np.arange(w)[:, None]
    ck = np.arange(w)[None, :]
    cs = np.clip(cq - NA_KW // 2, 0, w - NA_KW)
    toep = jnp.where((ck >= cs) & (ck < cs + NA_KW), toep, NEG_INF)
    n_dr = toep.shape[1]
    wide = toep.transpose(0, 2, 1, 3).reshape(B_HEADS, w, n_dr * w)
    outside = jnp.full((B_HEADS, w, NA_KROWS * w), NEG_INF, F32)
    tiles = []
    for case in range(ok_r.shape[0]):
        q_rows = []
        for rq in range(NA_QROWS):
            pieces, rk = [], 0
            while rk < NA_KROWS:
                end = rk + 1
                inside = bool(ok_r[case, rq, rk])
                while end < NA_KROWS and bool(ok_r[case, rq, end]) == inside and (
                        not inside or dr[case, rq, end] == dr[case, rq, end - 1] + 1):
                    end += 1
                d0 = int(dr[case, rq, rk])
                pieces.append(wide[:, :, d0 * w:(d0 + end - rk) * w] if inside
                              else outside[:, :, :(end - rk) * w])
                rk = end
            q_rows.append(jnp.concatenate(pieces, axis=-1))
        tiles.append(jnp.concatenate(q_rows, axis=1))
    return jnp.stack(tiles)


def _na_kernel(q_ref, k0_ref, k1_ref, k2_ref, v0_ref, v1_ref, v2_ref, bias_ref, o_ref):
    kt_all = jnp.concatenate([k0_ref[...], k1_ref[...], k2_ref[...]], axis=1)
    v_all = jnp.concatenate([v0_ref[...], v1_ref[...], v2_ref[...]], axis=0)
    for h in range(B_HEADS):
        sl = slice(h * HEAD_DIM, (h + 1) * HEAD_DIM)
        s = jnp.dot(q_ref[:, sl], kt_all[sl, :], preferred_element_type=F32)
        s = s + bias_ref[0, h]
        mx = jnp.max(s, axis=-1, keepdims=True)
        p = jnp.exp(s - mx)
        den = jnp.sum(p, axis=-1, keepdims=True)
        o = jnp.dot(p.astype(BF16), v_all[:, sl], preferred_element_type=F32) / den
        o_ref[:, sl] = o.astype(BF16)


def _na_attn(qb, kb, vb, rpb, bsz, seq):
    rows = seq // GRID_W
    nj = rows // NA_QROWS
    bias = _na_bias(rpb, rows)
    tq = NA_QROWS * GRID_W
    kblocks = NA_KROWS // NA_QROWS
    per_b = seq // tq
    qmap = lambda b, j: (b * per_b + j, 0)

    def kmap(i):
        return lambda b, j: (b * per_b + jnp.clip(j - 1, 0, nj - kblocks) + i, 0)

    case = lambda b, j: (jnp.where(j == 0, 0, jnp.where(j == nj - 1, 2, 1)), 0, 0, 0)
    kv = [pl.BlockSpec((tq, B_DIM), kmap(i)) for i in range(kblocks)]
    kv_t = [pl.BlockSpec((B_DIM, tq), (lambda f: (lambda b, j: f(b, j)[::-1]))(kmap(i))) for i in range(kblocks)]
    return pl.pallas_call(
        _na_kernel,
        grid=(bsz, nj),
        in_specs=[pl.BlockSpec((tq, B_DIM), qmap)] + kv_t + kv
                 + [pl.BlockSpec((1, B_HEADS, tq, NA_KROWS * GRID_W), case)],
        out_specs=pl.BlockSpec((tq, B_DIM), qmap),
        out_shape=jax.ShapeDtypeStruct(qb.shape, BF16),
        compiler_params=_params("parallel", "arbitrary"),
        name="neighborhood_attn",
    )(qb, kb, kb, kb, vb, vb, vb, bias)


def _post_ln_mlp(x, mix, g1, b1, w1_ref, w2_ref, g2, b2):
    x1 = _layer_norm(DN_ALPHA * x + mix, g1, b1)
    xb = x1.astype(BF16)
    acc = jnp.zeros_like(x1)
    fc = 1024
    for c in range(D_FF // fc):
        h = jnp.dot(xb, w1_ref[:, c * fc:(c + 1) * fc], preferred_element_type=F32)
        h = jnp.maximum(h, 0.0)
        acc = acc + jnp.dot((h * h).astype(BF16), w2_ref[c * fc:(c + 1) * fc, :],
                            preferred_element_type=F32)
    return _layer_norm(DN_ALPHA * x1 + acc, g2, b2)


def _attn_tail_kernel(oa_ref, ob_ref, x_ref, wo_ref, g1_ref, b1_ref, w1_ref, w2_ref, g2_ref, b2_ref,
                      out_ref):
    mix = (jnp.dot(oa_ref[...], wo_ref[:A_Q_DIM, :], preferred_element_type=F32)
           + jnp.dot(ob_ref[...], wo_ref[A_Q_DIM:, :], preferred_element_type=F32))
    out_ref[...] = _post_ln_mlp(x_ref[...], mix, g1_ref[...], b1_ref[...], w1_ref, w2_ref,
                                g2_ref[...], b2_ref[...])


def _attn_tail(oa, ob, x2d, wo, g1, b1, w1, w2, g2, b2):
    m = x2d.shape[0]
    tm = 1024
    row = lambda i: (i, 0)
    vec = lambda a: a.reshape(1, -1).astype(F32)
    return pl.pallas_call(
        _attn_tail_kernel,
        grid=(m // tm,),
        in_specs=[pl.BlockSpec((tm, A_Q_DIM), row), pl.BlockSpec((tm, B_DIM), row),
                  pl.BlockSpec((tm, D_MODEL), row), _resident(wo.shape),
                  _resident((1, D_MODEL)), _resident((1, D_MODEL)),
                  _resident(w1.shape), _resident(w2.shape),
                  _resident((1, D_MODEL)), _resident((1, D_MODEL))],
        out_specs=pl.BlockSpec((tm, D_MODEL), row),
        out_shape=jax.ShapeDtypeStruct((m, D_MODEL), F32),
        compiler_params=_params("parallel"),
        name="attn_out_mlp",
    )(oa, ob, x2d, wo, vec(g1), vec(b1), w1, w2, vec(g2), vec(b2))


def _ssm_tail_kernel(yf_ref, yb_ref, xs_ref, z_ref, dskip_ref, nw_ref, x_ref, wo_ref, g1_ref, b1_ref,
                     w1_ref, w2_ref, g2_ref, b2_ref, out_ref):
    y = yf_ref[...].astype(F32) + yb_ref[...].astype(F32) + xs_ref[...].astype(F32) * dskip_ref[...]
    y = y * z_ref[...].astype(F32)
    parts = []
    for g in range(SSM_GROUPS):
        yg = y[:, g * SSM_GROUP_W:(g + 1) * SSM_GROUP_W]
        ms = jnp.mean(yg * yg, axis=-1, keepdims=True)
        parts.append(yg * lax.rsqrt(ms + RMS_EPS))
    yn = (jnp.concatenate(parts, axis=-1) * nw_ref[...]).astype(BF16)
    mix = jnp.dot(yn, wo_ref[...], preferred_element_type=F32)
    out_ref[...] = _post_ln_mlp(x_ref[...], mix, g1_ref[...], b1_ref[...], w1_ref, w2_ref,
                                g2_ref[...], b2_ref[...])


def _ssm_tail(yf, yb, xbc, z, d_skip, norm_w, x2d, wo, g1, b1, w1, w2, g2, b2):
    m = x2d.shape[0]
    tm = 512
    row = lambda i: (i, 0)
    vec = lambda a: a.reshape(1, -1).astype(F32)
    wide = pl.BlockSpec((tm, SSM_D_INNER), row)
    dskip_e = jnp.repeat(d_skip.astype(F32), SSM_D_INNER // SSM_HEADS).reshape(1, -1)
    return pl.pallas_call(
        _ssm_tail_kernel,
        grid=(m // tm,),
        in_specs=[wide, wide, wide, wide, _resident((1, SSM_D_INNER)), _resident((1, SSM_D_INNER)),
                  pl.BlockSpec((tm, D_MODEL), row), _resident(wo.shape),
                  _resident((1, D_MODEL)), _resident((1, D_MODEL)),
                  _resident(w1.shape), _resident(w2.shape),
                  _resident((1, D_MODEL)), _resident((1, D_MODEL))],
        out_specs=pl.BlockSpec((tm, D_MODEL), row),
        out_shape=jax.ShapeDtypeStruct((m, D_MODEL), F32),
        compiler_params=_params("parallel"),
        name="ssm_out_mlp",
    )(yf, yb, xbc, z, dskip_e, vec(norm_w), x2d, wo, vec(g1), vec(b1), w1, w2, vec(g2), vec(b2))


def _ssm_in_kernel(tiles_per_seq, prev_ref, x_ref, next_ref, w_ref, cw_ref, cb_ref,
                   z_ref, xs_ref, bt_ref, c_ref, dt_ref, raw_ref, act_ref):
    tm = x_ref.shape[0]
    pos = pl.program_id(0) % tiles_per_seq
    x = x_ref[...]
    xb = x.astype(BF16)
    nc = 1024
    for c in range(SSM_D_INNER // nc):
        z_ref[:, c * nc:(c + 1) * nc] = _silu(jnp.dot(
            xb, w_ref[:, c * nc:(c + 1) * nc], preferred_element_type=F32)).astype(BF16)
    c0 = SSM_D_INNER + SSM_CONV_DIM
    dt_ref[...] = jnp.dot(xb, w_ref[:, c0:c0 + LANES], preferred_element_type=F32)
    prev = jnp.where(pos > 0, prev_ref[...], 0.0)
    nxt = jnp.where(pos < tiles_per_seq - 1, next_ref[...], 0.0)
    xe = jnp.concatenate([prev, x, nxt], axis=0).astype(BF16)
    pad = SSM_CONV // 2
    groups = tm // CONV_PHASES
    for c in range(SSM_CONV_DIM // nc):
        cols = slice(c * nc, (c + 1) * nc)
        raw = jnp.dot(xe, w_ref[:, SSM_D_INNER + c * nc:SSM_D_INNER + (c + 1) * nc],
                      preferred_element_type=F32)
        for j in range(nc // LANES):
            lanes = slice(c * nc + j * LANES, c * nc + (j + 1) * LANES)
            raw_ref[j] = raw[:, j * LANES:(j + 1) * LANES]
            slabs = {}
            for ph in range(CONV_PHASES):
                acc = cb_ref[:, lanes]
                for k in range(SSM_CONV):
                    m = HALO + ph + k - pad
                    if m not in slabs:
                        slabs[m] = raw_ref[j, pl.ds(m, groups, stride=CONV_PHASES), :]
                    acc = acc + slabs[m] * cw_ref[k:k + 1, lanes]
                act_ref[j, pl.ds(ph, groups, stride=CONV_PHASES), :] = _silu(acc)
            act = act_ref[j].astype(BF16)
            lo = c * nc + j * LANES
            if lo < SSM_D_INNER:
                xs_ref[:, lo:lo + LANES] = act
            elif lo < SSM_D_INNER + SSM_GROUPS * SSM_STATE:
                lo -= SSM_D_INNER
                bt_ref[lo:lo + LANES, :] = act.T
            else:
                lo -= SSM_D_INNER + SSM_GROUPS * SSM_STATE
                c_ref[:, lo:lo + LANES] = act


def _ssm_in_proj(x2d, w_pad, conv_w, conv_b, seq):
    m = x2d.shape[0]
    tm = 512
    tpb = seq // tm
    hb = tm // HALO
    last = m // HALO - 1
    n_bc = SSM_GROUPS * SSM_STATE
    row = lambda i: (i, 0)
    return pl.pallas_call(
        functools.partial(_ssm_in_kernel, tpb),
        grid=(m // tm,),
        in_specs=[pl.BlockSpec((HALO, D_MODEL), lambda i: (jnp.maximum(i * hb - 1, 0), 0)),
                  pl.BlockSpec((tm, D_MODEL), row),
                  pl.BlockSpec((HALO, D_MODEL), lambda i: (jnp.minimum((i + 1) * hb, last), 0)),
                  _resident(w_pad.shape), _resident((SSM_CONV, SSM_CONV_DIM)), _resident((1, SSM_CONV_DIM))],
        out_specs=[pl.BlockSpec((tm, SSM_D_INNER), row), pl.BlockSpec((tm, SSM_D_INNER), row),
                   pl.BlockSpec((n_bc, tm), lambda i: (0, i)), pl.BlockSpec((tm, n_bc), row),
                   pl.BlockSpec((tm, LANES), row)],
        out_shape=[jax.ShapeDtypeStruct((m, SSM_D_INNER), BF16),
                   jax.ShapeDtypeStruct((m, SSM_D_INNER), BF16),
                   jax.ShapeDtypeStruct((n_bc, m), BF16),
                   jax.ShapeDtypeStruct((m, n_bc), BF16),
                   jax.ShapeDtypeStruct((m, LANES), F32)],
        scratch_shapes=[pltpu.VMEM((1024 // LANES, tm + 2 * HALO, LANES), F32),
                        pltpu.VMEM((1024 // LANES, tm, LANES), F32)],
        compiler_params=_params("parallel"),
        name="ssm_in_conv",
    )(x2d, x2d, x2d, w_pad, conv_w.astype(F32), conv_b.reshape(1, -1).astype(F32))


def _ssd_chunk(reverse, xs_ref, bt_ref, c_ref, dtraw_ref, dtb_ref, alog_ref, y_ref,
               state_ref, blk_ref, rt_ref):
    L = SSM_CHUNK
    hpg = SSM_HEADS_PER_GROUP
    hw = SSM_GROUP_W // hpg
    off = SSM_HEADS if reverse else 0

    v = dtraw_ref[...] + dtb_ref[...]
    dt = jnp.maximum(v, 0.0) + jnp.log1p(jnp.exp(-jnp.abs(v)))
    a2 = dt * (-jnp.exp(alog_ref[...])) * LOG2E
    li = lax.broadcasted_iota(jnp.int32, (L, L), 0)
    si = lax.broadcasted_iota(jnp.int32, (L, L), 1)
    ac2 = a2
    s = 1
    while s < L:
        if reverse:
            ac2 = ac2 + jnp.where(li < L - s, pltpu.roll(ac2, L - s, 0), 0.0)
        else:
            ac2 = ac2 + jnp.where(li >= s, pltpu.roll(ac2, s, 0), 0.0)
        s *= 2
    tot2 = ac2[0:1, :] if reverse else ac2[L - 1:L, :]
    ldt = jnp.log2(dt)
    rt_ref[...] = (ac2 - ldt).T
    from_start = jnp.exp2(ac2)
    to_end = jnp.exp2(tot2 - ac2 + ldt)
    chunk_decay = jnp.exp2(tot2)
    tri = (si >= li) if reverse else (li >= si)

    lane = lax.broadcasted_iota(jnp.int32, (L, LANES), 1)

    def expand(src, g):
        if src.shape[0] != L:
            return jnp.concatenate(
                [jnp.broadcast_to(src[:, off + g * hpg + r:off + g * hpg + r + 1], (src.shape[0], hw))
                 for r in range(hpg)], axis=1)
        return jnp.concatenate(
            [jnp.take_along_axis(src, off + g * hpg + 2 * j + lane // hw, axis=1, mode="promise_in_bounds")
             for j in range(hpg // 2)], axis=1)

    for g in range(SSM_GROUPS):
        bg_t = bt_ref[g * SSM_STATE:(g + 1) * SSM_STATE, :]
        cg = c_ref[:, g * SSM_STATE:(g + 1) * SSM_STATE]
        cb = jnp.dot(cg, bg_t, preferred_element_type=F32)
        ms = []
        for r in range(hpg):
            h = off + g * hpg + r
            col = jnp.broadcast_to(ac2[:, h:h + 1], (L, L))
            lm = jnp.exp2(jnp.where(tri, col - rt_ref[h:h + 1, :], NEG_INF))
            ms.append((cb * lm).astype(BF16))
            c0 = g * SSM_GROUP_W + r * hw
            blk_ref[g, r * L:(r + 1) * L, r * hw:(r + 1) * hw] = xs_ref[:, c0:c0 + hw]
        prev = state_ref[g]
        y = jnp.dot(jnp.concatenate(ms, axis=1), blk_ref[g], preferred_element_type=F32)
        y = y + jnp.dot(cg, prev.astype(BF16), preferred_element_type=F32) * expand(from_start, g)
        y_ref[:, g * SSM_GROUP_W:(g + 1) * SSM_GROUP_W] = y.astype(y_ref.dtype)
        xg = xs_ref[:, g * SSM_GROUP_W:(g + 1) * SSM_GROUP_W].astype(F32)
        xd = (xg * expand(to_end, g)).astype(BF16)
        contrib = jnp.dot(bg_t, xd, preferred_element_type=F32)
        state_ref[g] = prev * expand(chunk_decay, g) + contrib


def _ssd_kernel(xf_ref, bf_ref, cf_ref, dtf_ref, xb_ref, bb_ref, cb_ref, dtb_ref, bias_ref, alog_ref,
                yf_ref, yb_ref, sf_ref, kf_ref, rf_ref, sb_ref, kb_ref, rb_ref):
    @pl.when(pl.program_id(1) == 0)
    def _():
        for ref in (sf_ref, kf_ref, sb_ref, kb_ref):
            ref[...] = jnp.zeros_like(ref)

    L = SSM_CHUNK
    for sub in range(SSD_CHUNKS_PER_STEP):
        f = slice(sub * L, (sub + 1) * L)
        b = slice((SSD_CHUNKS_PER_STEP - 1 - sub) * L, (SSD_CHUNKS_PER_STEP - sub) * L)
        _ssd_chunk(False, xf_ref.at[f, :], bf_ref.at[:, f], cf_ref.at[f, :], dtf_ref.at[f, :], bias_ref, alog_ref,
                   yf_ref.at[f, :], sf_ref, kf_ref.at[sub], rf_ref.at[sub])
        _ssd_chunk(True, xb_ref.at[b, :], bb_ref.at[:, b], cb_ref.at[b, :], dtb_ref.at[b, :], bias_ref, alog_ref,
                   yb_ref.at[b, :], sb_ref, kb_ref.at[sub], rb_ref.at[sub])


def _ssd_scan(xs, b_t, cm, dt_raw, dt_bias_row, alog_row, bsz, seq):
    m = xs.shape[0]
    L = SSM_CHUNK
    sub = SSD_CHUNKS_PER_STEP
    blk = sub * L
    nc = seq // blk
    n_bc = SSM_GROUPS * SSM_STATE
    fwd = lambda b, c: b * nc + c
    bwd = lambda b, c: b * nc + nc - 1 - c

    def operands(ch):
        return [pl.BlockSpec((blk, SSM_D_INNER), lambda b, c: (ch(b, c), 0)),
                pl.BlockSpec((n_bc, blk), lambda b, c: (0, ch(b, c))),
                pl.BlockSpec((blk, n_bc), lambda b, c: (ch(b, c), 0)),
                pl.BlockSpec((blk, LANES), lambda b, c: (ch(b, c), 0))]

    scratch = [pltpu.VMEM((SSM_GROUPS, SSM_STATE, SSM_GROUP_W), F32),
               pltpu.VMEM((sub, SSM_GROUPS, SSM_HEADS_PER_GROUP * L, SSM_GROUP_W), BF16),
               pltpu.VMEM((sub, LANES, L), F32)]
    y_shape = jax.ShapeDtypeStruct((m, SSM_D_INNER), BF16)
    return pl.pallas_call(
        _ssd_kernel,
        grid=(bsz, nc),
        in_specs=operands(fwd) + operands(bwd) + [_resident((1, LANES)), _resident((1, LANES))],
        out_specs=[pl.BlockSpec((blk, SSM_D_INNER), lambda b, c: (fwd(b, c), 0)),
                   pl.BlockSpec((blk, SSM_D_INNER), lambda b, c: (bwd(b, c), 0))],
        out_shape=[y_shape, y_shape],
        scratch_shapes=scratch + scratch,
        compiler_params=_params("parallel", "arbitrary"),
        name="ssd_scan",
    )(xs, b_t, cm, dt_raw, xs, b_t, cm, dt_raw, dt_bias_row, alog_row)


def kernel(x, attn_w_in, attn_sink, attn_rpb, attn_w_out, ssm_w_in, ssm_conv_w, ssm_conv_b, ssm_dt_bias,
           ssm_A_log, ssm_D, ssm_norm_w, ssm_w_out, mlp_w1, mlp_w2, ln1_g, ln1_b, ln2_g, ln2_b):
    bsz, seq, _ = x.shape
    x2d = x.reshape(bsz * seq, D_MODEL)

    qa, ka, va, qb, kb, vb = _attn_in_proj(x2d, attn_w_in[0].astype(BF16), seq)
    oa = _win_attn(qa, ka, va, attn_sink[0], bsz, seq)
    ob = _na_attn(qb, kb, vb, attn_rpb[0], bsz, seq)
    x2d = _attn_tail(oa, ob, x2d, attn_w_out[0].astype(BF16), ln1_g[0], ln1_b[0],
                     mlp_w1[0].astype(BF16), mlp_w2[0].astype(BF16), ln2_g[0], ln2_b[0])

    n_dt = 2 * SSM_HEADS
    w_pad = jnp.pad(ssm_w_in[0].astype(BF16), ((0, 0), (0, LANES - n_dt)))
    z, xs, b_t, cm, dt_raw = _ssm_in_proj(x2d, w_pad, ssm_conv_w[0], ssm_conv_b[0], seq)
    lane_pad = lambda a: jnp.pad(a.reshape(1, n_dt).astype(F32), ((0, 0), (0, LANES - n_dt)))
    y_f, y_b = _ssd_scan(xs, b_t, cm, dt_raw, lane_pad(ssm_dt_bias[0]), lane_pad(ssm_A_log[0]), bsz, seq)
    x2d = _ssm_tail(y_f, y_b, xs, z, ssm_D[0], ssm_norm_w[0], x2d, ssm_w_out[0].astype(BF16),
                    ln1_g[1], ln1_b[1], mlp_w1[1].astype(BF16), mlp_w2[1].astype(BF16),
                    ln2_g[1], ln2_b[1])
    return x2d.reshape(bsz, seq, D_MODEL)
```
